```python
import jax
import jax.numpy as jnp
from jax import lax
import numpy as np

D_MODEL = 1024
BATCH = 2
SEQ = 16384
DEPTH = 1
DEC_BATCH = 128
DEC_SEQ = 8
PAST_LEN = 8192
PAGE_SIZE = 128

Q_HEADS = 16
KV_HEADS = 4
GROUP = Q_HEADS // KV_HEADS
HEAD_DIM = D_MODEL // Q_HEADS
IDX_HEADS = 4
IDX_DIM = 64
IDX_WEIGHT_SCALE = (IDX_HEADS ** -0.5) * (IDX_DIM ** -0.5)
TOPK_MAX = 256
Q_BLOCK = 128
RET_HEADS = 4
RET_DK = D_MODEL // (2 * RET_HEADS)
RET_DV = D_MODEL // RET_HEADS
RET_CHUNK = 128
RET_DECAY_OFFSET = 5.0
ROPE_BASE = 10000.0
MEM_TOKENS = 256
MEM_HEADS = 4
MEM_DIM = D_MODEL // MEM_HEADS
N_BRANCHES = 3
FFN_HIDDEN = -(-8 * D_MODEL // (3 * 256)) * 256
EPS = 1e-6
POOL_NUM = 5
POOL_DEN = 4

IN_SIZES = (Q_HEADS * HEAD_DIM, KV_HEADS * HEAD_DIM, KV_HEADS * HEAD_DIM,
            IDX_HEADS * IDX_DIM, IDX_DIM, IDX_HEADS,
            RET_HEADS * RET_DK, RET_HEADS * RET_DK, RET_HEADS * RET_DV, RET_HEADS * RET_DV,
            MEM_HEADS * MEM_DIM, N_BRANCHES * D_MODEL)
IN_WIDTH = sum(IN_SIZES)
IN_OFFSETS = tuple(int(v) for v in np.cumsum(IN_SIZES)[:-1])

kernel_name = 'hybrid_dsa_retention_memory_step'


def _rmsnorm(x, g):
    xf = x.astype(jnp.float32)
    y = xf * lax.rsqrt(jnp.mean(xf * xf, axis=-1, keepdims=True) + EPS)
    return (y * g.astype(jnp.float32)).astype(x.dtype)


def _rotary(x, pos):
    half = x.shape[-1] // 2
    inv = ROPE_BASE ** (-jnp.arange(half, dtype=jnp.float32) / half)
    ang = pos.astype(jnp.float32)[:, None] * inv[None, :]
    cos = jnp.cos(ang)[None, :, None, :]
    sin = jnp.sin(ang)[None, :, None, :]
    x1, x2 = x[..., :half], x[..., half:]
    return jnp.concatenate([x1 * cos - x2 * sin, x1 * sin + x2 * cos], axis=-1)


def _log_gamma():
    return jnp.log(1.0 - jnp.exp2(-RET_DECAY_OFFSET - jnp.arange(RET_HEADS, dtype=jnp.float32)))


def _project(x, pos, attn_norm_g, w_in):
    B, T = x.shape[0], x.shape[1]
    h = _rmsnorm(x, attn_norm_g)
    q_a, k_a, v_a, q_i, k_i, w_i, q_r, k_r, v_r, g_r, q_m, gates = jnp.split(h @ w_in, IN_OFFSETS, axis=-1)
    f32 = jnp.float32
    q_a = q_a.reshape(B, T, Q_HEADS, HEAD_DIM)
    k_a = k_a.reshape(B, T, KV_HEADS, HEAD_DIM)
    v_a = v_a.reshape(B, T, KV_HEADS, HEAD_DIM)
    q_i = q_i.reshape(B, T, IDX_HEADS, IDX_DIM)
    w_i = w_i * IDX_WEIGHT_SCALE
    q_r = _rotary(q_r.reshape(B, T, RET_HEADS, RET_DK).astype(f32), pos)
    k_r = _rotary(k_r.reshape(B, T, RET_HEADS, RET_DK).astype(f32), pos) * RET_DK ** -0.5
    v_r = v_r.reshape(B, T, RET_HEADS, RET_DV).astype(f32)
    q_m = q_m.reshape(B, T, MEM_HEADS, MEM_DIM)
    gates = jax.nn.sigmoid(gates.astype(f32)).reshape(B, T, N_BRANCHES, D_MODEL)
    return q_a, k_a, v_a, q_i, k_i, w_i, q_r, k_r, v_r, g_r, q_m, gates


def _index_scores(q_i, w_i, k_i):
    s = jax.nn.relu(jnp.einsum('bthd,bld->bthl', q_i.astype(jnp.float32), k_i.astype(jnp.float32)))
    return jnp.einsum('bthl,bth->btl', s, w_i.astype(jnp.float32))


def _gather_rows(a, idx):
    return jax.vmap(lambda ab, ib: ab[ib])(a, idx)


def _gathered_attention(q, k_sel, v_sel, valid):
    B, T = q.shape[0], q.shape[1]
    qg = q.reshape(B, T, KV_HEADS, GROUP, HEAD_DIM)
    s = jnp.einsum('btngd,btknd->btngk', qg, k_sel).astype(jnp.float32) * HEAD_DIM ** -0.5
    s = jnp.where(valid[:, :, None, None, :], s, -jnp.inf)
    p = jax.nn.softmax(s, axis=-1).astype(v_sel.dtype)
    o = jnp.einsum('btngk,btknd->btngd', p, v_sel)
    return o.reshape(B, T, Q_HEADS * HEAD_DIM)


def _sparse_attention_prompt(q_a, k_a, v_a, q_i, w_i, k_i):
    B, S = q_a.shape[0], q_a.shape[1]
    n_blocks = S // Q_BLOCK
    top_k = min(TOPK_MAX, S // 4)
    key_pos = jnp.arange(S)

    def block(args):
        q_b, qi_b, wi_b, start = args
        q_pos = start + jnp.arange(Q_BLOCK)
        sc = _index_scores(qi_b, wi_b, k_i)
        sc = jnp.where((key_pos[None, :] <= q_pos[:, None])[None], sc, -jnp.inf)
        _, sel = lax.top_k(sc, top_k)
        valid = sel <= q_pos[None, :, None]
        return _gathered_attention(q_b, _gather_rows(k_a, sel), _gather_rows(v_a, sel), valid)

    def to_blocks(a):
        return jnp.moveaxis(a.reshape((B, n_blocks, Q_BLOCK) + a.shape[2:]), 1, 0)

    out = lax.map(block, (to_blocks(q_a), to_blocks(q_i), to_blocks(w_i), jnp.arange(n_blocks) * Q_BLOCK))
    return jnp.moveaxis(out, 0, 1).reshape(B, S, Q_HEADS * HEAD_DIM)


def _sparse_attention_sample(q_a, k_a, v_a, q_i, w_i, k_i, cache_k, cache_v, cache_idx_k, page_table):
    DB, T = q_a.shape[0], q_a.shape[1]
    past = page_table.shape[1] * PAGE_SIZE
    top_k = min(TOPK_MAX, (past + T) // 4)
    k_i_past = cache_idx_k[page_table].reshape(DB, past, IDX_DIM).astype(k_i.dtype)
    k_i_all = jnp.concatenate([k_i_past, k_i], axis=1)
    q_pos = past + jnp.arange(T)
    key_pos = jnp.arange(past + T)
    sc = _index_scores(q_i, w_i, k_i_all)
    sc = jnp.where((key_pos[None, :] <= q_pos[:, None])[None], sc, -jnp.inf)
    _, sel = lax.top_k(sc, top_k)
    in_past = (sel < past)[..., None, None]
    s_past = jnp.minimum(sel, past - 1)
    page = jnp.take_along_axis(page_table, (s_past // PAGE_SIZE).reshape(DB, -1), axis=1).reshape(sel.shape)
    phys = page * PAGE_SIZE + s_past % PAGE_SIZE
    s_new = jnp.clip(sel - past, 0, T - 1)
    k_sel = jnp.where(in_past, cache_k.reshape(-1, KV_HEADS, HEAD_DIM)[phys].astype(k_a.dtype), _gather_rows(k_a, s_new))
    v_sel = jnp.where(in_past, cache_v.reshape(-1, KV_HEADS, HEAD_DIM)[phys].astype(v_a.dtype), _gather_rows(v_a, s_new))
    valid = sel <= q_pos[None, :, None]
    return _gathered_attention(q_a, k_sel, v_sel, valid)


def _retention_chunk(state, q, k, v):
    C = q.shape[2]
    lg = _log_gamma()
    i = jnp.arange(C, dtype=jnp.float32)
    diff = i[:, None] - i[None, :]
    decay = jnp.where(diff >= 0, jnp.exp(lg[:, None, None] * jnp.maximum(diff, 0.0)), 0.0)
    inner = jnp.einsum('bhcd,bhsd->bhcs', q, k) * decay[None]
    q_dec = q * jnp.exp(lg[:, None] * (i + 1.0))[None, :, :, None]
    k_dec = k * jnp.exp(lg[:, None] * (C - 1.0 - i))[None, :, :, None]
    o = jnp.einsum('bhcs,bhse->bhce', inner, v) + jnp.einsum('bhcd,bhde->bhce', q_dec, state)
    new_state = jnp.exp(lg * C)[None, :, None, None] * state + jnp.einsum('bhcd,bhce->bhde', k_dec, v)
    return new_state, o


def _retention_prompt(q, k, v):
    B, S = q.shape[0], q.shape[1]
    n_chunks = S // RET_CHUNK

    def to_chunks(a):
        return jnp.moveaxis(a.reshape(B, n_chunks, RET_CHUNK, RET_HEADS, a.shape[-1]), (1, 3), (0, 2))

    state0 = jnp.zeros((B, RET_HEADS, RET_DK, RET_DV), jnp.float32)
    state, o = lax.scan(lambda s, c: _retention_chunk(s, *c), state0, (to_chunks(q), to_chunks(k), to_chunks(v)))
    o = jnp.moveaxis(o, (0, 2), (1, 3)).reshape(B, S, RET_HEADS, RET_DV)
    return state, o


def _retention_sample(state, q, k, v):
    new_state, o = _retention_chunk(state.astype(jnp.float32), jnp.swapaxes(q, 1, 2), jnp.swapaxes(k, 1, 2), jnp.swapaxes(v, 1, 2))
    return new_state, jnp.swapaxes(o, 1, 2)


def _retention_output(o, g_r, gn_g):
    B, T = o.shape[0], o.shape[1]
    mu = jnp.mean(o, axis=-1, keepdims=True)
    var = jnp.mean(jnp.square(o - mu), axis=-1, keepdims=True)
    on = ((o - mu) * lax.rsqrt(var + EPS)).reshape(B, T, RET_HEADS * RET_DV) * gn_g.astype(jnp.float32)
    return jax.nn.silu(g_r.astype(jnp.float32)) * on


def _memory_kv(mem, mem_norm_g, w_mem_kv):
    B, M = mem.shape[0], mem.shape[1]
    mk, mv = jnp.split(_rmsnorm(mem, mem_norm_g) @ w_mem_kv, 2, axis=-1)
    return mk.reshape(B, M, MEM_HEADS, MEM_DIM), mv.reshape(B, M, MEM_HEADS, MEM_DIM)


def _memory_attention(q_m, mem_k, mem_v):
    B, T = q_m.shape[0], q_m.shape[1]
    s = jnp.einsum('bthd,bmhd->bhtm', q_m, mem_k).astype(jnp.float32) * MEM_DIM ** -0.5
    p = jax.nn.softmax(s, axis=-1).astype(mem_v.dtype)
    return jnp.einsum('bhtm,bmhd->bthd', p, mem_v).reshape(B, T, MEM_HEADS * MEM_DIM)


def _merge_and_ffn(x, gates, o_a, o_r, o_m, w_out, ffn_norm_g, w_gate, w_up, w_down):
    u = gates[:, :, 0] * o_a.astype(jnp.float32) + gates[:, :, 1] * o_r + gates[:, :, 2] * o_m.astype(jnp.float32)
    x = x + (u.astype(x.dtype) @ w_out)
    h = _rmsnorm(x, ffn_norm_g)
    return x + (jax.nn.silu(h @ w_gate) * (h @ w_up)) @ w_down


def setup_inputs(seed: int = 0) -> dict:
    key = jax.random.key(seed)
    ks = jax.random.split(key, 24)
    n_pages = PAST_LEN // PAGE_SIZE
    n_pool = (DEC_BATCH * n_pages * POOL_NUM) // POOL_DEN

    def nrm(k, shape, scale=1.0):
        return jax.random.normal(k, shape, jnp.float32) * scale

    def gain(k, n):
        return 1.0 + nrm(k, (DEPTH, n), 0.01)

    page_table = jax.random.permutation(ks[0], n_pool)[: DEC_BATCH * n_pages].reshape(DEC_BATCH, n_pages).astype(jnp.int32)
    return {
        'x_prompt': nrm(ks[1], (BATCH, SEQ, D_MODEL)),
        'x_sample': nrm(ks[2], (DEC_BATCH, DEC_SEQ, D_MODEL)),
        'cache_k': nrm(ks[3], (DEPTH, n_pool, PAGE_SIZE, KV_HEADS, HEAD_DIM)),
        'cache_v': nrm(ks[4], (DEPTH, n_pool, PAGE_SIZE, KV_HEADS, HEAD_DIM)),
        'cache_idx_k': nrm(ks[5], (DEPTH, n_pool, PAGE_SIZE, IDX_DIM)),
        'state_ret': nrm(ks[6], (DEPTH, DEC_BATCH, RET_HEADS, RET_DK, RET_DV), 0.5),
        'cache_mem_k': nrm(ks[7], (DEPTH, DEC_BATCH, MEM_TOKENS, MEM_HEADS, MEM_DIM)),
        'cache_mem_v': nrm(ks[8], (DEPTH, DEC_BATCH, MEM_TOKENS, MEM_HEADS, MEM_DIM)),
        'page_table': page_table,
        'mem_prompt': nrm(ks[9], (BATCH, MEM_TOKENS, D_MODEL)),
        'attn_norm_g': gain(ks[10], D_MODEL),
        'w_in': nrm(ks[11], (DEPTH, D_MODEL, IN_WIDTH), D_MODEL ** -0.5),
        'ret_gn_g': gain(ks[12], RET_HEADS * RET_DV),
        'w_out': nrm(ks[13], (DEPTH, D_MODEL, D_MODEL), D_MODEL ** -0.5),
        'ffn_norm_g': gain(ks[14], D_MODEL),
        'w_gate': nrm(ks[15], (DEPTH, D_MODEL, FFN_HIDDEN), D_MODEL ** -0.5),
        'w_up': nrm(ks[16], (DEPTH, D_MODEL, FFN_HIDDEN), D_MODEL ** -0.5),
        'w_down': nrm(ks[17], (DEPTH, FFN_HIDDEN, D_MODEL), FFN_HIDDEN ** -0.5),
        'mem_norm_g': gain(ks[18], D_MODEL),
        'w_mem_kv': nrm(ks[19], (DEPTH, D_MODEL, 2 * MEM_HEADS * MEM_DIM), D_MODEL ** -0.5),
        'final_norm_g': 1.0 + nrm(ks[20], (D_MODEL,), 0.01),
    }


def reference(x_prompt, x_sample, cache_k, cache_v, cache_idx_k, state_ret, cache_mem_k, cache_mem_v, page_table, mem_prompt,
              attn_norm_g, w_in, ret_gn_g, w_out, ffn_norm_g, w_gate, w_up, w_down, mem_norm_g, w_mem_kv, final_norm_g):
    seq = x_prompt.shape[1]
    dec_seq = x_sample.shape[1]
    past = page_table.shape[1] * PAGE_SIZE
    pos_prompt = jnp.arange(seq)
    pos_sample = past + jnp.arange(dec_seq)
    xp, xs = x_prompt, x_sample
    k_p, v_p, ik_p, rs_p, mk_p, mv_p = [], [], [], [], [], []
    k_s, v_s, ik_s, rs_s = [], [], [], []
    for l in range(DEPTH):
        q_a, k_a, v_a, q_i, k_i, w_i, q_r, k_r, v_r, g_r, q_m, gates = _project(xp, pos_prompt, attn_norm_g[l], w_in[l])
        o_a = _sparse_attention_prompt(q_a, k_a, v_a, q_i, w_i, k_i)
        st, o_r = _retention_prompt(q_r, k_r, v_r)
        mem_k, mem_v = _memory_kv(mem_prompt, mem_norm_g[l], w_mem_kv[l])
        o_m = _memory_attention(q_m, mem_k, mem_v)
        xp = _merge_and_ffn(xp, gates, o_a, _retention_output(o_r, g_r, ret_gn_g[l]), o_m,
                            w_out[l], ffn_norm_g[l], w_gate[l], w_up[l], w_down[l])
        k_p.append(k_a)
        v_p.append(v_a)
        ik_p.append(k_i)
        rs_p.append(st)
        mk_p.append(mem_k)
        mv_p.append(mem_v)
        q_a, k_a, v_a, q_i, k_i, w_i, q_r, k_r, v_r, g_r, q_m, gates = _project(xs, pos_sample, attn_norm_g[l], w_in[l])
        o_a = _sparse_attention_sample(q_a, k_a, v_a, q_i, w_i, k_i, cache_k[l], cache_v[l], cache_idx_k[l], page_table)
        st, o_r = _retention_sample(state_ret[l], q_r, k_r, v_r)
        o_m = _memory_attention(q_m, cache_mem_k[l], cache_mem_v[l])
        xs = _merge_and_ffn(xs, gates, o_a, _retention_output(o_r, g_r, ret_gn_g[l]), o_m,
                            w_out[l], ffn_norm_g[l], w_gate[l], w_up[l], w_down[l])
        k_s.append(k_a)
        v_s.append(v_a)
        ik_s.append(k_i)
        rs_s.append(st)
    y_prompt = _rmsnorm(xp, final_norm_g)
    y_sample = _rmsnorm(xs, final_norm_g)
    return (y_prompt, y_sample, jnp.stack(k_p), jnp.stack(v_p), jnp.stack(ik_p), jnp.stack(rs_p), jnp.stack(mk_p), jnp.stack(mv_p),
            jnp.stack(k_s), jnp.stack(v_s), jnp.stack(ik_s), jnp.stack(rs_s))
```

```python
import functools
import math

import numpy as np
import jax
import jax.numpy as jnp
from jax import lax
from jax.experimental import pallas as pl
from jax.experimental.pallas import tpu as pltpu

F32 = jnp.float32
BF16 = jnp.bfloat16
I32 = jnp.int32

IDX_HEADS = 4
TOPK_MAX = 256
RET_CHUNK = 128
RET_DECAY_OFFSET = 5.0
ROPE_BASE = 10000.0
EPS = 1e-6

LANES = 128
VMEM_LIMIT_BYTES = 56 * 1024 * 1024

NEG = -1e30
INT_MIN = -(2 ** 31)

PROJ_ROWS = 256
ATTN_TQ = 128
ATTN_KB = 512
COUNT_CW = 512
MEM_ROWS = 512
SAMPLE_GROUP = 8


def _params(n_axes=1):
    return pltpu.CompilerParams(dimension_semantics=("arbitrary",) * n_axes,
                                vmem_limit_bytes=VMEM_LIMIT_BYTES)


def _whole_vmem():
    return pl.BlockSpec(memory_space=pltpu.VMEM)


def _rmsnorm(x, g):
    return (x * lax.rsqrt(jnp.mean(x * x, axis=-1, keepdims=True) + EPS)) * g


def _sigmoid(x):
    return 1.0 / (1.0 + jnp.exp(-x))


def _dot(a, b):
    return jnp.dot(a, b, preferred_element_type=F32)


def _dot_nt(a, b):
    return lax.dot_general(a, b, (((1,), (1,)), ((), ())), preferred_element_type=F32)


def _proj_layout(d_model, kv_w, idx_dim, ret_qk_w, ret_v_w, mem_w, n_gate):
    segs = [("qa", d_model), ("ka", kv_w), ("va", kv_w), ("qi", IDX_HEADS * idx_dim), ("kw", LANES),
            ("qr", ret_qk_w), ("kr", ret_qk_w), ("vr", ret_v_w), ("gr", ret_v_w), ("qm", mem_w),
            ("gt", n_gate)]
    out, off = {}, 0
    for name, w in segs:
        out[name] = (off, off + w)
        off += w
    return out, off


def _proj_kernel(x_ref, g_ref, w_ref, cos_ref, sin_ref,
                 qa_ref, ka_ref, va_ref, qi_ref, ki_ref, wi_ref, qr_ref, kr_ref, vr_ref, gr_ref, qm_ref, gt_ref,
                 *, lay, idx_dim, ret_dk, qa_scale, wi_scale, kr_scale, qm_scale):
    hb = _rmsnorm(x_ref[...], g_ref[...]).astype(BF16)

    def seg(name):
        lo, hi = lay[name]
        return _dot(hb, w_ref[:, lo:hi])

    qa_ref[...] = (seg("qa") * qa_scale).astype(qa_ref.dtype)
    ka_ref[...] = seg("ka")
    va_ref[...] = seg("va")
    qi_ref[...] = seg("qi").astype(qi_ref.dtype)
    kw = seg("kw")
    ki_ref[...] = kw[:, :idx_dim]
    wi_ref[...] = kw[:, idx_dim:idx_dim + IDX_HEADS] * wi_scale

    cos2 = cos_ref[...]
    sin2 = sin_ref[...]

    def rotary(z, out_ref, scale):
        for h in range(z.shape[1] // ret_dk):
            zh = z[:, h * ret_dk:(h + 1) * ret_dk]
            r = zh * cos2 + pltpu.roll(zh, ret_dk // 2, 1) * sin2
            out_ref[:, h * ret_dk:(h + 1) * ret_dk] = r if scale is None else r * scale

    rotary(seg("qr"), qr_ref, None)
    rotary(seg("kr"), kr_ref, kr_scale)
    vr_ref[...] = seg("vr")
    gr_ref[...] = seg("gr")
    qm_ref[...] = (seg("qm") * qm_scale).astype(qm_ref.dtype)
    gt_ref[...] = _sigmoid(seg("gt"))


def _project(x2d, g, w_packed, cos2, sin2, lay, dims, q_dtype):
    t, d = x2d.shape
    tm = min(PROJ_ROWS, t)
    n_pos_blocks = cos2.shape[0] // tm
    widths = {k: hi - lo for k, (lo, hi) in lay.items()}
    out_defs = [("qa", widths["qa"], q_dtype), ("ka", widths["ka"], F32), ("va", widths["va"], F32),
                ("qi", widths["qi"], q_dtype), ("ki", dims["idx_dim"], F32), ("wi", IDX_HEADS, F32),
                ("qr", widths["qr"], F32), ("kr", widths["kr"], F32), ("vr", widths["vr"], F32),
                ("gr", widths["gr"], F32), ("qm", widths["qm"], q_dtype), ("gt", widths["gt"], F32)]
    row = lambda w: pl.BlockSpec((tm, w), lambda i: (i, 0))
    kern = functools.partial(
        _proj_kernel, lay=lay, idx_dim=dims["idx_dim"], ret_dk=dims["ret_dk"],
        qa_scale=dims["head_dim"] ** -0.5, wi_scale=(IDX_HEADS ** -0.5) * (dims["idx_dim"] ** -0.5),
        kr_scale=dims["ret_dk"] ** -0.5, qm_scale=dims["mem_dim"] ** -0.5)
    outs = pl.pallas_call(
        kern,
        grid=(t // tm,),
        in_specs=[row(d), pl.BlockSpec((1, d), lambda i: (0, 0)), _whole_vmem(),
                  pl.BlockSpec((tm, LANES), lambda i: (i % n_pos_blocks, 0)),
                  pl.BlockSpec((tm, LANES), lambda i: (i % n_pos_blocks, 0))],
        out_specs=[row(w) for _, w, _ in out_defs],
        out_shape=[jax.ShapeDtypeStruct((t, w), dt) for _, w, dt in out_defs],
        compiler_params=_params(),
        name="proj",
    )(x2d, g.reshape(1, d), w_packed, cos2, sin2)
    return {name: o for (name, _, _), o in zip(out_defs, outs)}


def _rotary_tables(pos, ret_dk, rows):
    half = ret_dk // 2
    inv = ROPE_BASE ** (-jnp.arange(half, dtype=F32) / half)
    ang = pos.astype(F32)[:, None] * inv[None, :]
    cos, sin = jnp.cos(ang), jnp.sin(ang)
    cos2 = jnp.concatenate([cos, cos], axis=1)
    sin2 = jnp.concatenate([-sin, sin], axis=1)
    reps = max(1, rows // cos2.shape[0])
    return jnp.tile(cos2, (reps, 1)), jnp.tile(sin2, (reps, 1))


def _sortable_key(score):
    bits = lax.bitcast_convert_type(score, I32)
    return jnp.where(bits >= 0, bits, INT_MIN - bits)


def _index_scores(s4, w):
    acc = w[:, 0:1] * jnp.maximum(s4[0], 0.0)
    for h in range(1, IDX_HEADS):
        acc = acc + w[:, h:h + 1] * jnp.maximum(s4[h], 0.0)
    return acc


def _topk_threshold(keys_ref, rows, n_chunks, topk):
    def count_ge(cand):
        cb = jnp.broadcast_to(cand, (rows, LANES))

        def body(c, a):
            c0 = pl.multiple_of(c * COUNT_CW, COUNT_CW)
            blk = keys_ref[:, pl.ds(c0, COUNT_CW)]
            for u in range(COUNT_CW // LANES):
                a = a + jnp.where(blk[:, u * LANES:(u + 1) * LANES] >= cb, 1.0, 0.0)
            return a

        a = lax.fori_loop(0, n_chunks, body, jnp.zeros((rows, LANES), F32))
        return jnp.sum(a, axis=1, keepdims=True)

    def bit_step(b, u):
        cand_u = u | lax.shift_left(jnp.int32(1), 31 - b)
        cnt = count_ge(cand_u ^ INT_MIN)
        return jnp.where(cnt >= topk, cand_u, u)

    u = lax.fori_loop(0, 32, bit_step, jnp.zeros((rows, 1), I32))
    thr = u ^ INT_MIN
    n_greater = count_ge(thr + 1)
    need = jnp.where(thr == INT_MIN, 0.0, topk - n_greater)
    return thr, need


def _selection_bias(keys, thr, need, run, tri):
    eq = keys == thr
    gt = keys > thr
    eq_b = jnp.where(eq, 1.0, 0.0).astype(BF16)
    prefix = _dot(eq_b, tri) + run
    bias = jnp.where(gt, 0.0, jnp.where(eq, jnp.where(prefix <= need, 0.0, NEG), NEG))
    return bias, prefix[:, keys.shape[1] - 1:]


def _softmax_tile(s, p_rhs, m_ref, l_ref, acc_ref, idx):
    m_prev = m_ref[idx]
    m_new = jnp.maximum(m_prev, jnp.max(s, axis=1, keepdims=True))
    alpha = jnp.exp(m_prev - m_new)
    reps = s.shape[1] // LANES
    p = jnp.exp(s - (jnp.concatenate([m_new] * reps, axis=1) if reps > 1 else m_new))
    l_ref[idx] = alpha * l_ref[idx] + jnp.sum(p, axis=1, keepdims=True)
    acc = acc_ref[idx]
    a_reps = acc.shape[1] // LANES
    acc_ref[idx] = (jnp.concatenate([alpha] * a_reps, axis=1) if a_reps > 1 else alpha) * acc + _dot(p.astype(BF16), p_rhs)
    m_ref[idx] = m_new


def _sparse_prompt_kernel(qi_ref, wi_ref, qa_ref, kit_ref, kt_ref, v_ref, tri_ref, o_ref,
                          keys_ref, m_ref, l_ref, acc_ref, *, tq, kb, topk, kv_heads, group, head_dim):
    i = pl.program_id(0)
    n_tiles = ((i + 1) * tq + kb - 1) // kb
    row_pos = i * tq + lax.broadcasted_iota(I32, (tq, kb), 0)
    col_iota = lax.broadcasted_iota(I32, (tq, kb), 1)
    w = wi_ref[...]
    qi = qi_ref[...].reshape(IDX_HEADS * tq, qi_ref.shape[2])

    def score_tile(j, c):
        k0 = pl.multiple_of(j * kb, kb)
        s4 = _dot(qi, kit_ref[:, pl.ds(k0, kb)]).reshape(IDX_HEADS, tq, kb)
        key = _sortable_key(_index_scores(s4, w))
        keys_ref[:, pl.ds(k0, kb)] = jnp.where(k0 + col_iota <= row_pos, key, INT_MIN)
        return c

    lax.fori_loop(0, n_tiles, score_tile, 0)

    thr, need = _topk_threshold(keys_ref, tq, n_tiles * (kb // COUNT_CW), topk)

    m_ref[...] = jnp.full(m_ref.shape, NEG, F32)
    l_ref[...] = jnp.zeros(l_ref.shape, F32)
    acc_ref[...] = jnp.zeros(acc_ref.shape, F32)

    def attn_tile(j, run):
        k0 = pl.multiple_of(j * kb, kb)
        bias, run = _selection_bias(keys_ref[:, pl.ds(k0, kb)], thr, need, run, tri_ref[...])
        for n in range(kv_heads):
            qn = qa_ref[n * group:(n + 1) * group].reshape(group * tq, head_dim)
            s = _dot(qn, kt_ref[n * head_dim:(n + 1) * head_dim, pl.ds(k0, kb)])
            s = (s.reshape(group, tq, kb) + bias[None]).reshape(group * tq, kb)
            lane0 = (n * head_dim) // LANES * LANES
            _softmax_tile(s, v_ref[pl.ds(k0, kb), lane0:lane0 + LANES], m_ref, l_ref, acc_ref, n)
        return run

    lax.fori_loop(0, n_tiles, attn_tile, jnp.zeros((tq, 1), F32))

    for n in range(kv_heads):
        o = acc_ref[n] * (1.0 / l_ref[n])
        off = (n * head_dim) % LANES
        for g in range(group):
            h = n * group + g
            o_ref[:, h * head_dim:(h + 1) * head_dim] = o[g * tq:(g + 1) * tq, off:off + head_dim]


def _sparse_attention_prompt(qi, wi, qa, kit, kt, v, tri, topk, kv_heads, head_dim):
    q_heads, s, _ = qa.shape
    tq, kb = min(ATTN_TQ, s), min(ATTN_KB, s)
    group = q_heads // kv_heads
    kern = functools.partial(_sparse_prompt_kernel, tq=tq, kb=kb, topk=topk, kv_heads=kv_heads,
                             group=group, head_dim=head_dim)
    return pl.pallas_call(
        kern,
        grid=(s // tq,),
        in_specs=[pl.BlockSpec((IDX_HEADS, tq, qi.shape[2]), lambda i: (0, i, 0)),
                  pl.BlockSpec((tq, IDX_HEADS), lambda i: (i, 0)),
                  pl.BlockSpec((q_heads, tq, head_dim), lambda i: (0, i, 0)),
                  _whole_vmem(), _whole_vmem(), _whole_vmem(), _whole_vmem()],
        out_specs=pl.BlockSpec((tq, q_heads * head_dim), lambda i: (i, 0)),
        out_shape=jax.ShapeDtypeStruct((s, q_heads * head_dim), F32),
        scratch_shapes=[pltpu.VMEM((tq, s), I32),
                        pltpu.VMEM((kv_heads, group * tq, LANES), F32),
                        pltpu.VMEM((kv_heads, group * tq, LANES), F32),
                        pltpu.VMEM((kv_heads, group * tq, LANES), F32)],
        compiler_params=_params(),
        name="sparse_attn_prompt",
    )(qi, wi, qa, kit, kt, v, tri)


def _sparse_sample_kernel(pt_ref, qi_ref, wi_ref, qbd_ref, kin_ref, kan_ref, van_ref, tri_ref,
                          cidx_hbm, ck_hbm, cv_hbm, o_ref,
                          ibuf, kbuf, vbuf, sems, keys_ref, m_ref, l_ref, acc_ref,
                          *, t, kb, topk, n_pages, page, kv_heads, group, head_dim):
    b = pl.program_id(0)
    nb = pl.num_programs(0)
    past = n_pages * page
    rows = kv_heads * group * t

    def page_copies(bb, slot, p):
        pg = pt_ref[bb, p]
        dst = pl.ds(pl.multiple_of(p * page, page), page)
        return (pltpu.make_async_copy(cidx_hbm.at[pg], ibuf.at[slot, dst], sems.at[slot, 0]),
                pltpu.make_async_copy(ck_hbm.at[pg], kbuf.at[slot, dst], sems.at[slot, 1]),
                pltpu.make_async_copy(cv_hbm.at[pg], vbuf.at[slot, dst], sems.at[slot, 2]))

    def start_fetch(bb, slot):
        def body(p, c):
            for cp in page_copies(bb, slot, p):
                cp.start()
            return c
        lax.fori_loop(0, n_pages, body, 0)

    def wait_fetch(bb, slot):
        def body(p, c):
            for cp in page_copies(bb, slot, p):
                cp.wait()
            return c
        lax.fori_loop(0, n_pages, body, 0)

    slot = b % 2

    @pl.when(b == 0)
    def _():
        start_fetch(b, slot)

    @pl.when(b + 1 < nb)
    def _():
        start_fetch(b + 1, 1 - slot)

    wait_fetch(b, slot)

    w = wi_ref[0]
    qi = qi_ref[0]
    n_tiles = past // kb

    def score_tile(j, c):
        k0 = pl.multiple_of(j * kb, kb)
        kc = ibuf[slot, pl.ds(k0, kb), :].astype(BF16)
        s4 = _dot_nt(qi, kc).reshape(IDX_HEADS, t, kb)
        keys_ref[:, pl.ds(k0, kb)] = _sortable_key(_index_scores(s4, w))
        return c

    lax.fori_loop(0, n_tiles, score_tile, 0)
    s4 = _dot_nt(qi, kin_ref[0]).reshape(IDX_HEADS, t, LANES)
    key_new = _sortable_key(_index_scores(s4, w))
    new_col = lax.broadcasted_iota(I32, (t, LANES), 1)
    new_row = lax.broadcasted_iota(I32, (t, LANES), 0)
    keys_ref[:, past:past + LANES] = jnp.where(new_col <= new_row, key_new, INT_MIN)
    if COUNT_CW > LANES:
        keys_ref[:, past + LANES:past + COUNT_CW] = jnp.full((t, COUNT_CW - LANES), INT_MIN, I32)

    thr, need = _topk_threshold(keys_ref, t, past // COUNT_CW + 1, topk)

    m_ref[...] = jnp.full(m_ref.shape, NEG, F32)
    l_ref[...] = jnp.zeros(l_ref.shape, F32)
    acc_ref[...] = jnp.zeros(acc_ref.shape, F32)
    qbd = qbd_ref[0]

    def tile_bias(bias, width):
        return jnp.broadcast_to(bias[None], (rows // t, t, width)).reshape(rows, width)

    def attn_tile(j, run):
        k0 = pl.multiple_of(j * kb, kb)
        bias, run = _selection_bias(keys_ref[:, pl.ds(k0, kb)], thr, need, run, tri_ref[...])
        s = _dot_nt(qbd, kbuf[slot, pl.ds(k0, kb), :].astype(BF16)) + tile_bias(bias, kb)
        _softmax_tile(s, vbuf[slot, pl.ds(k0, kb), :].astype(BF16), m_ref, l_ref, acc_ref, 0)
        return run

    run = lax.fori_loop(0, n_tiles, attn_tile, jnp.zeros((t, 1), F32))
    bias, _ = _selection_bias(keys_ref[:, past:past + LANES], thr, need, run, tri_ref[:LANES, :LANES])
    s = _dot_nt(qbd, kan_ref[0]) + tile_bias(bias, LANES)
    _softmax_tile(s, van_ref[0], m_ref, l_ref, acc_ref, 0)

    l = l_ref[0]
    o = acc_ref[0] * (1.0 / jnp.concatenate([l] * (acc_ref.shape[2] // LANES), axis=1))
    for n in range(kv_heads):
        for g in range(group):
            h = n * group + g
            r0 = (n * group + g) * t
            o_ref[0, :, h * head_dim:(h + 1) * head_dim] = o[r0:r0 + t, n * head_dim:(n + 1) * head_dim]


def _sparse_attention_sample(page_table, qi, wi, qbd, ki_new, ka_new, va_new, tri, cache_idx, cache_k, cache_v,
                             topk, kv_heads, group, head_dim):
    db, t, _ = wi.shape
    n_pages = page_table.shape[1]
    page = cache_idx.shape[1]
    past = n_pages * page
    kb = min(ATTN_KB, past)
    rows = kv_heads * group * t
    kvw = kv_heads * head_dim
    kern = functools.partial(_sparse_sample_kernel, t=t, kb=kb, topk=topk, n_pages=n_pages, page=page,
                             kv_heads=kv_heads, group=group, head_dim=head_dim)
    per_b = lambda shape: pl.BlockSpec((1,) + shape, lambda b, pt: (b, 0, 0))
    grid_spec = pltpu.PrefetchScalarGridSpec(
        num_scalar_prefetch=1,
        grid=(db,),
        in_specs=[per_b(qi.shape[1:]), per_b(wi.shape[1:]), per_b(qbd.shape[1:]), per_b(ki_new.shape[1:]),
                  per_b(ka_new.shape[1:]), per_b(va_new.shape[1:]),
                  pl.BlockSpec(tri.shape, lambda b, pt: (0, 0)),
                  pl.BlockSpec(memory_space=pl.ANY), pl.BlockSpec(memory_space=pl.ANY),
                  pl.BlockSpec(memory_space=pl.ANY)],
        out_specs=per_b((t, kv_heads * group * head_dim)),
        scratch_shapes=[pltpu.VMEM((2, past, cache_idx.shape[2]), F32),
                        pltpu.VMEM((2, past, kvw), F32),
                        pltpu.VMEM((2, past, kvw), F32),
                        pltpu.SemaphoreType.DMA((2, 3)),
                        pltpu.VMEM((t, past + COUNT_CW), I32),
                        pltpu.VMEM((1, rows, LANES), F32),
                        pltpu.VMEM((1, rows, LANES), F32),
                        pltpu.VMEM((1, rows, kvw), F32)])
    return pl.pallas_call(
        kern,
        grid_spec=grid_spec,
        out_shape=jax.ShapeDtypeStruct((db, t, kv_heads * group * head_dim), F32),
        compiler_params=_params(),
        name="sparse_attn_sample",
    )(page_table, qi, wi, qbd, ki_new, ka_new, va_new, tri, cache_idx, cache_k, cache_v)


def _log_gamma(heads):
    return np.log(1.0 - np.exp2(-RET_DECAY_OFFSET - np.arange(heads, dtype=np.float64)))


def _retention_tables(heads, chunk, dk, group):
    lg = _log_gamma(heads)
    i = np.arange(chunk, dtype=np.float64)
    diff = i[:, None] - i[None, :]
    decay = np.where(diff >= 0, np.exp(lg[:, None, None] * np.maximum(diff, 0.0)), 0.0)
    decay_bd = np.zeros((heads, group * chunk, group * chunk))
    for gidx in range(group):
        decay_bd[:, gidx * chunk:(gidx + 1) * chunk, gidx * chunk:(gidx + 1) * chunk] = decay
    q_scale = np.tile(np.exp(lg[:, None] * (i + 1.0)), (1, group))
    k_scale = np.tile(np.exp(lg[:, None] * (chunk - 1.0 - i)), (1, group))
    state_scale = np.exp(lg * chunk)
    rows_q = np.broadcast_to(q_scale[:, :, None], (heads, group * chunk, dk))
    rows_k = np.broadcast_to(k_scale[:, :, None], (heads, group * chunk, dk))
    cols_k = np.broadcast_to(k_scale[:, None, :], (heads, dk, group * chunk))
    as32 = lambda a: jnp.asarray(np.ascontiguousarray(a), F32)
    return as32(decay_bd), as32(rows_q), as32(rows_k), as32(cols_k), [float(v) for v in state_scale]


def _groupnorm_gate(o, gn_g, g_r):
    mu = jnp.mean(o, axis=-1, keepdims=True)
    d = o - mu
    var = jnp.mean(d * d, axis=-1, keepdims=True)
    return (g_r * _sigmoid(g_r)) * ((d * lax.rsqrt(var + EPS)) * gn_g)


def _retention_prompt_kernel(q_ref, k_ref, v_ref, g_ref, gn_ref, dec_ref, qs_ref, ks_ref, o_ref, st_ref, state,
                             *, heads, dk, dv, state_scale):
    c = pl.program_id(1)

    @pl.when(c == 0)
    def _():
        state[...] = jnp.zeros(state.shape, F32)

    for h in range(heads):
        q = q_ref[0, :, h * dk:(h + 1) * dk]
        k = k_ref[0, :, h * dk:(h + 1) * dk]
        v = v_ref[0, :, h * dv:(h + 1) * dv].astype(BF16)
        s_old = state[h]
        inner = _dot_nt(q.astype(BF16), k.astype(BF16)) * dec_ref[h]
        o = _dot(inner.astype(BF16), v) + _dot((q * qs_ref[h]).astype(BF16), s_old.astype(BF16))
        state[h] = state_scale[h] * s_old + _dot((k * ks_ref[h]).T.astype(BF16), v)
        o_ref[0, :, h * dv:(h + 1) * dv] = _groupnorm_gate(o, gn_ref[:, h * dv:(h + 1) * dv],
                                                           g_ref[0, :, h * dv:(h + 1) * dv])

    @pl.when(c == pl.num_programs(1) - 1)
    def _():
        st_ref[0] = state[...]


def _retention_prompt(q, k, v, g, gn_g, heads, dk, dv):
    b, s, _ = q.shape
    chunk = min(RET_CHUNK, s)
    dec, qs, ks, _, state_scale = _retention_tables(heads, chunk, dk, 1)
    blk = lambda w: pl.BlockSpec((1, chunk, w), lambda bi, ci: (bi, ci, 0))
    const = lambda a: pl.BlockSpec(a.shape, lambda bi, ci: (0,) * a.ndim)
    gn2 = gn_g.reshape(1, heads * dv)
    kern = functools.partial(_retention_prompt_kernel, heads=heads, dk=dk, dv=dv, state_scale=state_scale)
    return pl.pallas_call(
        kern,
        grid=(b, s // chunk),
        in_specs=[blk(heads * dk), blk(heads * dk), blk(heads * dv), blk(heads * dv),
                  const(gn2), const(dec), const(qs), const(ks)],
        out_specs=[blk(heads * dv), pl.BlockSpec((1, heads, dk, dv), lambda bi, ci: (bi, 0, 0, 0))],
        out_shape=[jax.ShapeDtypeStruct((b, s, heads * dv), F32),
                   jax.ShapeDtypeStruct((b, heads, dk, dv), F32)],
        scratch_shapes=[pltpu.VMEM((heads, dk, dv), F32)],
        compiler_params=_params(2),
        name="retention_prompt",
    )(q, k, v, g, gn2, dec, qs, ks)


def _retention_sample_kernel(q_ref, k_ref, kt_ref, v_ref, g_ref, gn_ref, dec_ref, qs_ref, kst_ref, st_ref,
                             o_ref, sto_ref, *, heads, dk, dv, t, group, state_scale):
    rows = group * t
    col_batch = lax.broadcasted_iota(I32, (dk, rows), 1) // t
    for h in range(heads):
        q = q_ref[:, h * dk:(h + 1) * dk]
        k = k_ref[:, h * dk:(h + 1) * dk]
        v = v_ref[:, h * dv:(h + 1) * dv].astype(BF16)
        inner = _dot_nt(q.astype(BF16), k.astype(BF16)) * dec_ref[h]
        o = _dot(inner.astype(BF16), v)
        qd = (q * qs_ref[h]).astype(BF16)
        kdt = kt_ref[0, h] * kst_ref[h]
        cross = []
        for bi in range(group):
            s_old = st_ref[bi, h]
            cross.append(_dot(qd, s_old.astype(BF16))[bi * t:(bi + 1) * t])
            kd_b = jnp.where(col_batch == bi, kdt, 0.0).astype(BF16)
            sto_ref[bi, h] = state_scale[h] * s_old + _dot(kd_b, v)
        o = o + jnp.concatenate(cross, axis=0)
        o_ref[:, h * dv:(h + 1) * dv] = _groupnorm_gate(o, gn_ref[:, h * dv:(h + 1) * dv],
                                                        g_ref[:, h * dv:(h + 1) * dv])


def _retention_sample(state, q, k, v, g, gn_g, heads, dk, dv, t):
    db = state.shape[0]
    group = min(SAMPLE_GROUP, db)
    rows = group * t
    dec, qs, _, kst, state_scale = _retention_tables(heads, t, dk, group)
    kt = k.reshape(db // group, rows, heads, dk).transpose(0, 2, 3, 1)
    gn2 = gn_g.reshape(1, heads * dv)
    blk = lambda w: pl.BlockSpec((rows, w), lambda i: (i, 0))
    const = lambda a: pl.BlockSpec(a.shape, lambda i: (0,) * a.ndim)
    st_spec = pl.BlockSpec((group, heads, dk, dv), lambda i: (i, 0, 0, 0))
    kern = functools.partial(_retention_sample_kernel, heads=heads, dk=dk, dv=dv, t=t, group=group,
                             state_scale=state_scale)
    return pl.pallas_call(
        kern,
        grid=(db // group,),
        in_specs=[blk(heads * dk), blk(heads * dk),
                  pl.BlockSpec((1, heads, dk, rows), lambda i: (i, 0, 0, 0)),
                  blk(heads * dv), blk(heads * dv), const(gn2), const(dec), const(qs), const(kst), st_spec],
        out_specs=[blk(heads * dv), st_spec],
        out_shape=[jax.ShapeDtypeStruct((db * t, heads * dv), F32),
                   jax.ShapeDtypeStruct(state.shape, F32)],
        compiler_params=_params(),
        name="retention_sample",
    )(q, k, kt, v, g, gn2, dec, qs, kst, state)


def _mem_kv_kernel(x_ref, g_ref, w_ref, k_ref, v_ref):
    hb = _rmsnorm(x_ref[...], g_ref[...]).astype(BF16)
    half = w_ref.shape[1] // 2
    k_ref[...] = _dot(hb, w_ref[:, :half])
    v_ref[...] = _dot(hb, w_ref[:, half:])


def _memory_kv(mem2d, g, w_bf16):
    t, d = mem2d.shape
    half = w_bf16.shape[1] // 2
    tm = min(PROJ_ROWS, t)
    return pl.pallas_call(
        _mem_kv_kernel,
        grid=(t // tm,),
        in_specs=[pl.BlockSpec((tm, d), lambda i: (i, 0)), pl.BlockSpec((1, d), lambda i: (0, 0)), _whole_vmem()],
        out_specs=[pl.BlockSpec((tm, half), lambda i: (i, 0))] * 2,
        out_shape=[jax.ShapeDtypeStruct((t, half), F32)] * 2,
        compiler_params=_params(),
        name="memory_kv",
    )(mem2d, g.reshape(1, d), w_bf16)


def _mem_head_attention(q, k, v):
    s = _dot_nt(q, k)
    p = jnp.exp(s - jnp.max(s, axis=1, keepdims=True))
    return _dot(p.astype(BF16), v) * (1.0 / jnp.sum(p, axis=1, keepdims=True))


def _mem_attn_prompt_kernel(q_ref, k_ref, v_ref, o_ref, *, heads, dim):
    for h in range(heads):
        sl = slice(h * dim, (h + 1) * dim)
        o_ref[0, :, sl] = _mem_head_attention(q_ref[0, :, sl], k_ref[0, :, sl], v_ref[0, :, sl])


def _memory_attention_prompt(q, mk, mv, heads, dim):
    b, s, w = q.shape
    tm = min(MEM_ROWS, s)
    m = mk.shape[1]
    return pl.pallas_call(
        functools.partial(_mem_attn_prompt_kernel, heads=heads, dim=dim),
        grid=(b, s // tm),
        in_specs=[pl.BlockSpec((1, tm, w), lambda bi, i: (bi, i, 0)),
                  pl.BlockSpec((1, m, w), lambda bi, i: (bi, 0, 0)),
                  pl.BlockSpec((1, m, w), lambda bi, i: (bi, 0, 0))],
        out_specs=pl.BlockSpec((1, tm, w), lambda bi, i: (bi, i, 0)),
        out_shape=jax.ShapeDtypeStruct((b, s, w), F32),
        compiler_params=_params(2),
        name="memory_attn_prompt",
    )(q, mk, mv)


def _mem_attn_sample_kernel(q_ref, k_ref, v_ref, o_ref, *, heads, dim, t, group):
    for bi in range(group):
        for h in range(heads):
            sl = slice(h * dim, (h + 1) * dim)
            o = _mem_head_attention(q_ref[:, sl].astype(BF16), k_ref[bi, :, sl].astype(BF16),
                                    v_ref[bi, :, sl].astype(BF16))
            o_ref[bi * t:(bi + 1) * t, sl] = o[bi * t:(bi + 1) * t]


def _memory_attention_sample(q, mk, mv, heads, dim, t):
    db, m, w = mk.shape
    group = min(SAMPLE_GROUP, db)
    return pl.pallas_call(
        functools.partial(_mem_attn_sample_kernel, heads=heads, dim=dim, t=t, group=group),
        grid=(db // group,),
        in_specs=[pl.BlockSpec((group * t, w), lambda i: (i, 0)),
                  pl.BlockSpec((group, m, w), lambda i: (i, 0, 0)),
                  pl.BlockSpec((group, m, w), lambda i: (i, 0, 0))],
        out_specs=pl.BlockSpec((group * t, w), lambda i: (i, 0)),
        out_shape=jax.ShapeDtypeStruct((db * t, w), F32),
        compiler_params=_params(),
        name="memory_attn_sample",
    )(q, mk, mv)


def _merge_ffn_kernel(x_ref, gt_ref, oa_ref, or_ref, om_ref, wo_ref, fg_ref, wg_ref, wu_ref, wd_ref, fin_ref,
                      y_ref, *, final_norm):
    d = x_ref.shape[1]
    u = (gt_ref[:, :d] * oa_ref[...] + gt_ref[:, d:2 * d] * or_ref[...] + gt_ref[:, 2 * d:] * om_ref[...])
    x1 = x_ref[...] + _dot(u.astype(BF16), wo_ref[...])
    hb = _rmsnorm(x1, fg_ref[...]).astype(BF16)
    a = _dot(hb, wg_ref[...])
    act = (a * _sigmoid(a)) * _dot(hb, wu_ref[...])
    x2 = x1 + _dot(act.astype(BF16), wd_ref[...])
    y_ref[...] = _rmsnorm(x2, fin_ref[...]) if final_norm else x2


def _merge_ffn(x2d, gates, o_a, o_r, o_m, w_out, ffn_g, w_gate, w_up, w_down, final_g, final_norm):
    t, d = x2d.shape
    tm = min(PROJ_ROWS, t)
    row = lambda w: pl.BlockSpec((tm, w), lambda i: (i, 0))
    vec = pl.BlockSpec((1, d), lambda i: (0, 0))
    return pl.pallas_call(
        functools.partial(_merge_ffn_kernel, final_norm=final_norm),
        grid=(t // tm,),
        in_specs=[row(d), row(gates.shape[1]), row(d), row(d), row(d), _whole_vmem(), vec,
                  _whole_vmem(), _whole_vmem(), _whole_vmem(), vec],
        out_specs=row(d),
        out_shape=jax.ShapeDtypeStruct((t, d), F32),
        compiler_params=_params(),
        name="merge_ffn",
    )(x2d, gates, o_a, o_r, o_m, w_out, ffn_g.reshape(1, d), w_gate, w_up, w_down, final_g.reshape(1, d))


def kernel(x_prompt, x_sample, cache_k, cache_v, cache_idx_k, state_ret, cache_mem_k, cache_mem_v, page_table,
           mem_prompt, attn_norm_g, w_in, ret_gn_g, w_out, ffn_norm_g, w_gate, w_up, w_down, mem_norm_g, w_mem_kv,
           final_norm_g):
    depth = w_in.shape[0]
    b, s, d = x_prompt.shape
    db, t, _ = x_sample.shape
    _, n_pool, page, kv_heads, head_dim = cache_k.shape
    idx_dim = cache_idx_k.shape[-1]
    _, _, ret_heads, ret_dk, ret_dv = state_ret.shape
    _, _, mem_tokens, mem_heads, mem_dim = cache_mem_k.shape
    q_heads = d // head_dim
    group = q_heads // kv_heads
    kvw = kv_heads * head_dim
    n_pages = page_table.shape[1]
    past = n_pages * page
    dims = dict(idx_dim=idx_dim, ret_dk=ret_dk, head_dim=head_dim, mem_dim=mem_dim)
    lay, _ = _proj_layout(d, kvw, idx_dim, ret_heads * ret_dk, ret_heads * ret_dv, mem_heads * mem_dim,
                          w_in.shape[2] - (d + 2 * kvw + IDX_HEADS * idx_dim + idx_dim + IDX_HEADS
                                           + 2 * ret_heads * ret_dk + 2 * ret_heads * ret_dv + mem_heads * mem_dim))
    o_ki = d + 2 * kvw + IDX_HEADS * idx_dim
    o_qr = o_ki + idx_dim + IDX_HEADS

    cos_p, sin_p = _rotary_tables(jnp.arange(s), ret_dk, min(PROJ_ROWS, b * s))
    cos_s, sin_s = _rotary_tables(past + jnp.arange(t), ret_dk, min(PROJ_ROWS, db * t))
    kb_p, kb_s = min(ATTN_KB, s), min(ATTN_KB, past)
    tri_n = max(kb_p, kb_s, LANES)
    tri = (jnp.arange(tri_n)[:, None] <= jnp.arange(tri_n)[None, :]).astype(BF16)
    topk_p = min(TOPK_MAX, s // 4)
    topk_s = min(TOPK_MAX, (past + t) // 4)

    xp = x_prompt.reshape(b * s, d)
    xs = x_sample.reshape(db * t, d)
    outs = {k: [] for k in ("k_p", "v_p", "ik_p", "rs_p", "mk_p", "mv_p", "k_s", "v_s", "ik_s", "rs_s")}
    for l in range(depth):
        last = l == depth - 1
        wl = w_in[l]
        w_packed = jnp.concatenate(
            [wl[:, :o_ki], wl[:, o_ki:o_qr], jnp.zeros((d, LANES - (o_qr - o_ki)), wl.dtype), wl[:, o_qr:]],
            axis=1).astype(BF16)
        w_out_b, w_gate_b, w_up_b, w_down_b = (a[l].astype(BF16) for a in (w_out, w_gate, w_up, w_down))

        p = _project(xp, attn_norm_g[l], w_packed, cos_p, sin_p, lay, dims, BF16)
        qi = p["qi"].reshape(b, s, IDX_HEADS, idx_dim).transpose(0, 2, 1, 3)
        qa = p["qa"].reshape(b, s, q_heads, head_dim).transpose(0, 2, 1, 3)
        wi = p["wi"].reshape(b, s, IDX_HEADS)
        kit = p["ki"].astype(BF16).reshape(b, s, idx_dim).transpose(0, 2, 1)
        kt = p["ka"].astype(BF16).reshape(b, s, kvw).transpose(0, 2, 1)
        vb = p["va"].astype(BF16).reshape(b, s, kvw)
        o_a = jnp.stack([
            _sparse_attention_prompt(qi[bi], wi[bi], qa[bi], kit[bi], kt[bi], vb[bi], tri[:kb_p, :kb_p],
                                     topk_p, kv_heads, head_dim) for bi in range(b)]).reshape(b * s, d)
        o_r, st_p = _retention_prompt(p["qr"].reshape(b, s, -1), p["kr"].reshape(b, s, -1),
                                      p["vr"].reshape(b, s, -1), p["gr"].reshape(b, s, -1), ret_gn_g[l],
                                      ret_heads, ret_dk, ret_dv)
        mk, mv = _memory_kv(mem_prompt.reshape(b * mem_tokens, d), mem_norm_g[l], w_mem_kv[l].astype(BF16))
        o_m = _memory_attention_prompt(p["qm"].reshape(b, s, -1), mk.astype(BF16).reshape(b, mem_tokens, -1),
                                       mv.astype(BF16).reshape(b, mem_tokens, -1), mem_heads, mem_dim)
        xp = _merge_ffn(xp, p["gt"], o_a, o_r.reshape(b * s, -1), o_m.reshape(b * s, -1), w_out_b, ffn_norm_g[l],
                        w_gate_b, w_up_b, w_down_b, final_norm_g, last)
        outs["k_p"].append(p["ka"].reshape(b, s, kv_heads, head_dim))
        outs["v_p"].append(p["va"].reshape(b, s, kv_heads, head_dim))
        outs["ik_p"].append(p["ki"].reshape(b, s, idx_dim))
        outs["rs_p"].append(st_p)
        outs["mk_p"].append(mk.reshape(b, mem_tokens, mem_heads, mem_dim))
        outs["mv_p"].append(mv.reshape(b, mem_tokens, mem_heads, mem_dim))

        q = _project(xs, attn_norm_g[l], w_packed, cos_s, sin_s, lay, dims, F32)
        qi_s = q["qi"].reshape(db, t, IDX_HEADS, idx_dim).transpose(0, 2, 1, 3).reshape(db, IDX_HEADS * t, idx_dim)
        qa_s = q["qa"].reshape(db, t, kv_heads, group, head_dim).transpose(0, 2, 3, 1, 4)
        qbd = (qa_s.reshape(db, kv_heads, group * t, 1, head_dim)
               * jnp.eye(kv_heads, dtype=F32)[None, :, None, :, None]).reshape(db, kv_heads * group * t, kvw)
        pad_rows = lambda a: jnp.pad(a.reshape(db, t, -1), ((0, 0), (0, LANES - t), (0, 0))).astype(BF16)
        o_a = _sparse_attention_sample(
            page_table, qi_s.astype(BF16), q["wi"].reshape(db, t, IDX_HEADS), qbd.astype(BF16),
            pad_rows(q["ki"]), pad_rows(q["ka"]), pad_rows(q["va"]), tri[:max(kb_s, LANES), :max(kb_s, LANES)],
            cache_idx_k[l], cache_k[l].reshape(n_pool, page, kvw), cache_v[l].reshape(n_pool, page, kvw),
            topk_s, kv_heads, group, head_dim).reshape(db * t, d)
        o_r, st_s = _retention_sample(state_ret[l], q["qr"], q["kr"], q["vr"], q["gr"], ret_gn_g[l],
                                      ret_heads, ret_dk, ret_dv, t)
        o_m = _memory_attention_sample(q["qm"], cache_mem_k[l].reshape(db, mem_tokens, -1),
                                       cache_mem_v[l].reshape(db, mem_tokens, -1), mem_heads, mem_dim, t)
        xs = _merge_ffn(xs, q["gt"], o_a, o_r, o_m, w_out_b, ffn_norm_g[l], w_gate_b, w_up_b, w_down_b,
                        final_norm_g, last)
        outs["k_s"].append(q["ka"].reshape(db, t, kv_heads, head_dim))
        outs["v_s"].append(q["va"].reshape(db, t, kv_heads, head_dim))
        outs["ik_s"].append(q["ki"].reshape(db, t, idx_dim))
        outs["rs_s"].append(st_s)

    stk = lambda k: jnp.stack(outs[k])
    return (xp.reshape(b, s, d), xs.reshape(db, t, d), stk("k_p"), stk("v_p"), stk("ik_p"), stk("rs_p"),
            stk("mk_p"), stk("mv_p"), stk("k_s"), stk("v_s"), stk("ik_s"), stk("rs_s"))
```

```python
import functools
import math

import numpy as np
import jax
import jax.numpy as jnp
from jax import lax
from jax.experimental import pallas as pl
from jax.experimental.pallas import tpu as pltpu

F32 = jnp.float32
BF16 = jnp.bfloat16
I32 = jnp.int32

IDX_HEADS = 4
TOPK_MAX = 256
RET_CHUNK = 128
RET_DECAY_OFFSET = 5.0
ROPE_BASE = 10000.0
EPS = 1e-6

LANES = 128
VMEM_LIMIT_BYTES = 56 * 1024 * 1024

NEG = -1e30
INT_MIN = -(2 ** 31)
LOG2E = math.log2(math.e)
MIN_NORMAL = 2.0 ** -126

PROJ_ROWS = 256
ATTN_TQ = 128
ATTN_KB = 512
SAMPLE_KB = 1024
COUNT_ELEMS = 65536
TRI_W = 256
MEM_ROWS = 512
SAMPLE_GROUP = 8


def _params(n_axes=1):
    return pltpu.CompilerParams(dimension_semantics=("arbitrary",) * n_axes,
                                vmem_limit_bytes=VMEM_LIMIT_BYTES)


def _whole_vmem():
    return pl.BlockSpec(memory_space=pltpu.VMEM)


def _rmsnorm(x, g):
    return (x * lax.rsqrt(jnp.mean(x * x, axis=-1, keepdims=True) + EPS)) * g


def _sigmoid(x):
    return 1.0 / (1.0 + jnp.exp(-x))


def _dot(a, b):
    return jnp.dot(a, b, preferred_element_type=F32)


def _dot_nt(a, b):
    return lax.dot_general(a, b, (((1,), (1,)), ((), ())), preferred_element_type=F32)


def _proj_layout(d_model, kv_w, idx_dim, ret_qk_w, ret_v_w, mem_w, n_gate):
    segs = [("qa", d_model), ("ka", kv_w), ("va", kv_w), ("qi", IDX_HEADS * idx_dim), ("kw", LANES),
            ("qr", ret_qk_w), ("kr", ret_qk_w), ("vr", ret_v_w), ("gr", ret_v_w), ("qm", mem_w),
            ("gt", n_gate)]
    out, off = {}, 0
    for name, w in segs:
        out[name] = (off, off + w)
        off += w
    return out, off


def _proj_kernel(x_ref, g_ref, w_ref, cos_ref, sin_ref,
                 qa_ref, ka_ref, va_ref, qi_ref, ki_ref, wi_ref, qr_ref, kr_ref, vr_ref, gr_ref, qm_ref, gt_ref,
                 *, lay, idx_dim, ret_dk, qa_scale, wi_scale, kr_scale, qm_scale):
    hb = _rmsnorm(x_ref[...], g_ref[...]).astype(BF16)

    def seg(name):
        lo, hi = lay[name]
        return _dot(hb, w_ref[:, lo:hi])

    qa_ref[...] = (seg("qa") * qa_scale).astype(qa_ref.dtype)
    ka_ref[...] = seg("ka")
    va_ref[...] = seg("va")
    qi_ref[...] = seg("qi").astype(qi_ref.dtype)
    kw = seg("kw")
    ki_ref[...] = kw[:, :idx_dim]
    wi_ref[...] = kw[:, idx_dim:idx_dim + IDX_HEADS] * wi_scale

    cos2 = cos_ref[...]
    sin2 = sin_ref[...]

    def rotary(z, out_ref, scale):
        for h in range(z.shape[1] // ret_dk):
            zh = z[:, h * ret_dk:(h + 1) * ret_dk]
            r = zh * cos2 + pltpu.roll(zh, ret_dk // 2, 1) * sin2
            out_ref[:, h * ret_dk:(h + 1) * ret_dk] = r if scale is None else r * scale

    rotary(seg("qr"), qr_ref, None)
    rotary(seg("kr"), kr_ref, kr_scale)
    vr_ref[...] = seg("vr")
    gr_ref[...] = seg("gr")
    qm_ref[...] = (seg("qm") * qm_scale).astype(qm_ref.dtype)
    gt_ref[...] = _sigmoid(seg("gt"))


def _project(x2d, g, w_packed, cos2, sin2, lay, dims, q_dtype):
    t, d = x2d.shape
    tm = min(PROJ_ROWS, t)
    n_pos_blocks = cos2.shape[0] // tm
    widths = {k: hi - lo for k, (lo, hi) in lay.items()}
    out_defs = [("qa", widths["qa"], q_dtype), ("ka", widths["ka"], F32), ("va", widths["va"], F32),
                ("qi", widths["qi"], q_dtype), ("ki", dims["idx_dim"], F32), ("wi", IDX_HEADS, F32),
                ("qr", widths["qr"], F32), ("kr", widths["kr"], F32), ("vr", widths["vr"], F32),
                ("gr", widths["gr"], F32), ("qm", widths["qm"], q_dtype), ("gt", widths["gt"], F32)]
    row = lambda w: pl.BlockSpec((tm, w), lambda i: (i, 0))
    kern = functools.partial(
        _proj_kernel, lay=lay, idx_dim=dims["idx_dim"], ret_dk=dims["ret_dk"],
        qa_scale=dims["head_dim"] ** -0.5 * LOG2E, wi_scale=(IDX_HEADS ** -0.5) * (dims["idx_dim"] ** -0.5),
        kr_scale=dims["ret_dk"] ** -0.5, qm_scale=dims["mem_dim"] ** -0.5)
    outs = pl.pallas_call(
        kern,
        grid=(t // tm,),
        in_specs=[row(d), pl.BlockSpec((1, d), lambda i: (0, 0)), _whole_vmem(),
                  pl.BlockSpec((tm, LANES), lambda i: (i % n_pos_blocks, 0)),
                  pl.BlockSpec((tm, LANES), lambda i: (i % n_pos_blocks, 0))],
        out_specs=[row(w) for _, w, _ in out_defs],
        out_shape=[jax.ShapeDtypeStruct((t, w), dt) for _, w, dt in out_defs],
        compiler_params=_params(),
        name="proj",
    )(x2d, g.reshape(1, d), w_packed, cos2, sin2)
    return {name: o for (name, _, _), o in zip(out_defs, outs)}


def _rotary_tables(pos, ret_dk, rows):
    half = ret_dk // 2
    inv = ROPE_BASE ** (-jnp.arange(half, dtype=F32) / half)
    ang = pos.astype(F32)[:, None] * inv[None, :]
    cos, sin = jnp.cos(ang), jnp.sin(ang)
    cos2 = jnp.concatenate([cos, cos], axis=1)
    sin2 = jnp.concatenate([-sin, sin], axis=1)
    reps = max(1, rows // cos2.shape[0])
    return jnp.tile(cos2, (reps, 1)), jnp.tile(sin2, (reps, 1))


def _index_scores(s4, w):
    acc = w[:, 0:1] * jnp.maximum(s4[0], 0.0)
    for h in range(1, IDX_HEADS):
        acc = acc + w[:, h:h + 1] * jnp.maximum(s4[h], 0.0)
    return jnp.where(jnp.abs(acc) < MIN_NORMAL, 0.0, acc)


def _sortable_key(bits):
    return jnp.where(bits >= 0, bits, bits ^ 0x7FFFFFFF)


def _high_half(bits):
    return lax.bitcast_convert_type(bits & jnp.int32(-65536), F32)


def _count_chunk(rows):
    return max(LANES, COUNT_ELEMS // rows)


def _topk_threshold(keys_ref, hi_ref, rows, n_chunks, topk):
    cw = _count_chunk(rows)

    def tree_sum(parts):
        while len(parts) > 1:
            parts = [a + b for a, b in zip(parts[::2], parts[1::2])] + parts[len(parts) & ~1:]
        return parts[0]

    def count_ge(cand):
        cb = jnp.broadcast_to(cand, (rows, LANES))

        def body(c, a):
            blk = keys_ref[:, pl.ds(pl.multiple_of(c * cw, cw), cw)]
            return a + tree_sum([jnp.where(blk[:, u * LANES:(u + 1) * LANES] >= cb, 1.0, 0.0)
                                 for u in range(cw // LANES)])

        a = lax.fori_loop(0, n_chunks, body, jnp.zeros((rows, LANES), F32))
        return jnp.sum(a, axis=1, keepdims=True)

    def count_hi_ge(cand):
        cb = jnp.broadcast_to(cand, (rows, LANES)).astype(BF16)
        one, zero = jnp.ones((), BF16), jnp.zeros((), BF16)

        def body(c, a):
            blk = hi_ref[:, pl.ds(pl.multiple_of(c * cw, cw), cw)]
            return a + tree_sum([jnp.where(blk[:, u * LANES:(u + 1) * LANES] >= cb, one, zero)
                                 for u in range(cw // LANES)])

        a = lax.fori_loop(0, n_chunks, body, jnp.zeros((rows, LANES), BF16))
        return jnp.sum(a.astype(F32), axis=1, keepdims=True)

    def hi_step(b, u16):
        cand_u = u16 | lax.shift_left(jnp.int32(1), 15 - b)
        k16 = cand_u - 32768
        b16 = jnp.where(k16 < 0, k16 ^ 0x7FFF, k16) & 0xFFFF
        val = lax.bitcast_convert_type(lax.shift_left(b16, 16), F32)
        mag = b16 & 0x7FFF
        negative = b16 >= 0x8000
        val = jnp.where(mag >= 0x7F80, jnp.where(negative, -jnp.inf, jnp.inf), val)
        val = jnp.where(mag < 0x0080, jnp.where(negative, 0.0, jnp.where(mag == 0, 0.0, MIN_NORMAL)), val)
        return jnp.where(count_hi_ge(val) >= topk, cand_u, u16)

    def lo_step(b, u):
        cand_u = u | lax.shift_left(jnp.int32(1), b)
        return jnp.where(count_ge(cand_u ^ INT_MIN) >= topk, cand_u, u)

    zero = jnp.zeros((rows, 1), I32)
    if hi_ref is None:
        u = lax.fori_loop(0, 32, lambda b, u: lo_step(31 - b, u), zero)
    else:
        u16 = lax.fori_loop(0, 16, hi_step, zero)
        u = lax.fori_loop(0, 16, lambda b, u: lo_step(15 - b, u), lax.shift_left(u16, 16))
    thr = u ^ INT_MIN
    need = topk - count_ge(thr + 1)
    return thr, need


def _selection_bias(keys, thr, need, run, tri):
    tw = tri.shape[0]
    eq = keys == thr
    gt = keys > thr
    eq_b = jnp.where(eq, 1.0, 0.0).astype(BF16)
    parts = []
    for c in range(keys.shape[1] // tw):
        pre = _dot(eq_b[:, c * tw:(c + 1) * tw], tri) + run
        run = pre[:, tw - 1:]
        parts.append(pre)
    prefix = parts[0] if len(parts) == 1 else jnp.concatenate(parts, axis=1)
    bias = jnp.where(gt, 0.0, jnp.where(eq, jnp.where(prefix <= need, 0.0, NEG), NEG))
    return bias, run


def _lane_tile(x, width):
    reps = width // x.shape[1]
    return x if reps == 1 else jnp.concatenate([x] * reps, axis=1)


def _softmax_tile(s, pv, m_ref, l_ref, acc_ref, idx):
    m_prev = m_ref[idx]
    m_new = jnp.maximum(m_prev, jnp.max(s, axis=1, keepdims=True))
    alpha = jnp.exp2(m_prev - m_new)
    p = jnp.exp2(s - _lane_tile(m_new, s.shape[1]))
    l_ref[idx] = alpha * l_ref[idx] + jnp.sum(p, axis=1, keepdims=True)
    acc = acc_ref[idx]
    acc_ref[idx] = _lane_tile(alpha, acc.shape[1]) * acc + pv(p.astype(BF16))
    m_ref[idx] = m_new


def _sparse_prompt_kernel(qi_ref, wi_ref, qa_ref, kit_ref, kt_ref, v_ref, tri_ref, o_ref,
                          keys_ref, hi_ref, m_ref, l_ref, acc_ref, *, tq, kb, topk, kv_heads, group, head_dim):
    i = pl.program_id(1)
    n_tiles = ((i + 1) * tq + kb - 1) // kb
    row_pos = i * tq + lax.broadcasted_iota(I32, (tq, kb), 0)
    col_iota = lax.broadcasted_iota(I32, (tq, kb), 1)
    w = wi_ref[...]
    qi = qi_ref[...].reshape(IDX_HEADS * tq, qi_ref.shape[2])

    def score_tile(j, c):
        k0 = pl.multiple_of(j * kb, kb)
        s4 = _dot(qi, kit_ref[0, :, pl.ds(k0, kb)]).reshape(IDX_HEADS, tq, kb)
        bits = lax.bitcast_convert_type(_index_scores(s4, w), I32)
        visible = k0 + col_iota <= row_pos
        keys_ref[:, pl.ds(k0, kb)] = jnp.where(visible, _sortable_key(bits), INT_MIN)
        hi_ref[:, pl.ds(k0, kb)] = jnp.where(visible, _high_half(bits), -jnp.inf).astype(BF16)
        return c

    lax.fori_loop(0, n_tiles, score_tile, 0)

    cw = _count_chunk(tq)
    thr, need = _topk_threshold(keys_ref, hi_ref, tq, n_tiles * (kb // cw), topk)

    m_ref[...] = jnp.full(m_ref.shape, NEG, F32)
    l_ref[...] = jnp.zeros(l_ref.shape, F32)
    acc_ref[...] = jnp.zeros(acc_ref.shape, F32)

    def attn_tile(j, run):
        k0 = pl.multiple_of(j * kb, kb)
        bias, run = _selection_bias(keys_ref[:, pl.ds(k0, kb)], thr, need, run, tri_ref[...])
        for n in range(kv_heads):
            qn = qa_ref[n * group:(n + 1) * group].reshape(group * tq, head_dim)
            s = _dot(qn, kt_ref[0, n * head_dim:(n + 1) * head_dim, pl.ds(k0, kb)])
            s = (s.reshape(group, tq, kb) + bias[None]).reshape(group * tq, kb)
            lane0 = (n * head_dim) // LANES * LANES
            vn = v_ref[0, pl.ds(k0, kb), lane0:lane0 + LANES]
            _softmax_tile(s, lambda p: _dot(p, vn), m_ref, l_ref, acc_ref, n)
        return run

    lax.fori_loop(0, n_tiles, attn_tile, jnp.zeros((tq, 1), F32))

    for n in range(kv_heads):
        o = acc_ref[n] * (1.0 / l_ref[n])
        off = (n * head_dim) % LANES
        for g in range(group):
            h = n * group + g
            o_ref[:, h * head_dim:(h + 1) * head_dim] = o[g * tq:(g + 1) * tq, off:off + head_dim]


def _sparse_attention_prompt(qi, wi, qa, kit, kt, v, tri, topk, kv_heads, head_dim):
    q_heads = qa.shape[0]
    b, s, kvw = v.shape
    tq, kb = min(ATTN_TQ, s), min(ATTN_KB, s)
    nq = s // tq
    group = q_heads // kv_heads
    kern = functools.partial(_sparse_prompt_kernel, tq=tq, kb=kb, topk=topk, kv_heads=kv_heads,
                             group=group, head_dim=head_dim)
    resident = lambda shape: pl.BlockSpec((1,) + shape, lambda bi, i: (bi, 0, 0), pipeline_mode=pl.Buffered(1))
    return pl.pallas_call(
        kern,
        grid=(b, nq),
        in_specs=[pl.BlockSpec((IDX_HEADS, tq, qi.shape[2]), lambda bi, i: (0, bi * nq + i, 0)),
                  pl.BlockSpec((tq, IDX_HEADS), lambda bi, i: (bi * nq + i, 0)),
                  pl.BlockSpec((q_heads, tq, head_dim), lambda bi, i: (0, bi * nq + i, 0)),
                  resident(kit.shape[1:]), resident(kt.shape[1:]), resident(v.shape[1:]), _whole_vmem()],
        out_specs=pl.BlockSpec((tq, q_heads * head_dim), lambda bi, i: (bi * nq + i, 0)),
        out_shape=jax.ShapeDtypeStruct((b * s, q_heads * head_dim), F32),
        scratch_shapes=[pltpu.VMEM((tq, s), I32),
                        pltpu.VMEM((tq, s), BF16),
                        pltpu.VMEM((kv_heads, group * tq, LANES), F32),
                        pltpu.VMEM((kv_heads, group * tq, LANES), F32),
                        pltpu.VMEM((kv_heads, group * tq, LANES), F32)],
        compiler_params=_params(2),
        name="sparse_attn_prompt",
    )(qi, wi, qa, kit, kt, v, tri)


def _sparse_sample_kernel(pt_ref, qi_ref, wi_ref, qbd_ref, kin_ref, kan_ref, van_ref, tri_ref,
                          cidx_hbm, ck_hbm, cv_hbm, o_ref,
                          ibuf, kbuf, vbuf, sems, keys_ref, m_ref, l_ref, acc_ref,
                          *, t, kb, topk, n_pages, page, kv_heads, group, head_dim):
    b = pl.program_id(0)
    nb = pl.num_programs(0)
    past = n_pages * page
    rows = kv_heads * group * t
    cw = _count_chunk(t)

    def page_copies(bb, slot, p):
        pg = pt_ref[bb, p]
        dst = pl.ds(pl.multiple_of(p * page, page), page)
        return (pltpu.make_async_copy(cidx_hbm.at[pg], ibuf.at[slot, :, dst], sems.at[slot, 0]),
                pltpu.make_async_copy(ck_hbm.at[pg], kbuf.at[slot, :, dst], sems.at[slot, 1]),
                pltpu.make_async_copy(cv_hbm.at[pg], vbuf.at[slot, :, dst], sems.at[slot, 2]))

    def start_fetch(bb, slot):
        def body(p, c):
            for cp in page_copies(bb, slot, p):
                cp.start()
            return c
        lax.fori_loop(0, n_pages, body, 0)

    def wait_fetch(bb, slot):
        def body(p, c):
            for cp in page_copies(bb, slot, p):
                cp.wait()
            return c
        lax.fori_loop(0, n_pages, body, 0)

    slot = b % 2

    @pl.when(b == 0)
    def _():
        start_fetch(b, slot)

    @pl.when(b + 1 < nb)
    def _():
        start_fetch(b + 1, 1 - slot)

    wait_fetch(b, slot)

    w = wi_ref[0]
    qi = qi_ref[0]
    n_tiles = past // kb

    def score_tile(j, c):
        k0 = pl.multiple_of(j * kb, kb)
        s4 = _dot(qi, ibuf[slot, :, pl.ds(k0, kb)].astype(BF16)).reshape(IDX_HEADS, t, kb)
        keys_ref[:, pl.ds(k0, kb)] = _sortable_key(lax.bitcast_convert_type(_index_scores(s4, w), I32))
        return c

    lax.fori_loop(0, n_tiles, score_tile, 0)
    s4 = _dot_nt(qi, kin_ref[0]).reshape(IDX_HEADS, t, LANES)
    key_new = _sortable_key(lax.bitcast_convert_type(_index_scores(s4, w), I32))
    new_col = lax.broadcasted_iota(I32, (t, LANES), 1)
    new_row = lax.broadcasted_iota(I32, (t, LANES), 0)
    keys_ref[:, past:past + LANES] = jnp.where(new_col <= new_row, key_new, INT_MIN)
    keys_ref[:, past + LANES:past + cw] = jnp.full((t, cw - LANES), INT_MIN, I32)

    thr, need = _topk_threshold(keys_ref, None, t, past // cw + 1, topk)

    m_ref[...] = jnp.full(m_ref.shape, NEG, F32)
    l_ref[...] = jnp.zeros(l_ref.shape, F32)
    acc_ref[...] = jnp.zeros(acc_ref.shape, F32)
    qbd = qbd_ref[0]

    def tile_bias(bias, width):
        return jnp.broadcast_to(bias[None], (rows // t, t, width)).reshape(rows, width)

    def attn_tile(j, run):
        k0 = pl.multiple_of(j * kb, kb)
        bias, run = _selection_bias(keys_ref[:, pl.ds(k0, kb)], thr, need, run, tri_ref[...])
        s = _dot(qbd, kbuf[slot, :, pl.ds(k0, kb)].astype(BF16)) + tile_bias(bias, kb)
        vt = vbuf[slot, :, pl.ds(k0, kb)].astype(BF16)
        _softmax_tile(s, lambda p: _dot_nt(p, vt), m_ref, l_ref, acc_ref, 0)
        return run

    run = lax.fori_loop(0, n_tiles, attn_tile, jnp.zeros((t, 1), F32))
    bias, _ = _selection_bias(keys_ref[:, past:past + LANES], thr, need, run, tri_ref[:LANES, :LANES])
    s = _dot_nt(qbd, kan_ref[0]) + tile_bias(bias, LANES)
    van = van_ref[0]
    _softmax_tile(s, lambda p: _dot(p, van), m_ref, l_ref, acc_ref, 0)

    o = acc_ref[0] * (1.0 / _lane_tile(l_ref[0], acc_ref.shape[2]))
    for n in range(kv_heads):
        for g in range(group):
            h = n * group + g
            r0 = (n * group + g) * t
            o_ref[0, :, h * head_dim:(h + 1) * head_dim] = o[r0:r0 + t, n * head_dim:(n + 1) * head_dim]


def _sparse_attention_sample(page_table, qi, wi, qbd, ki_new, ka_new, va_new, tri, cache_idx_t, cache_k_t, cache_v_t,
                             topk, kv_heads, group, head_dim):
    db, t, _ = wi.shape
    n_pages = page_table.shape[1]
    page = cache_idx_t.shape[2]
    past = n_pages * page
    kb = min(SAMPLE_KB, past)
    rows = kv_heads * group * t
    kvw = kv_heads * head_dim
    cw = _count_chunk(t)
    kern = functools.partial(_sparse_sample_kernel, t=t, kb=kb, topk=topk, n_pages=n_pages, page=page,
                             kv_heads=kv_heads, group=group, head_dim=head_dim)
    per_b = lambda shape: pl.BlockSpec((1,) + shape, lambda b, pt: (b, 0, 0))
    grid_spec = pltpu.PrefetchScalarGridSpec(
        num_scalar_prefetch=1,
        grid=(db,),
        in_specs=[per_b(qi.shape[1:]), per_b(wi.shape[1:]), per_b(qbd.shape[1:]), per_b(ki_new.shape[1:]),
                  per_b(ka_new.shape[1:]), per_b(va_new.shape[1:]),
                  pl.BlockSpec(tri.shape, lambda b, pt: (0, 0)),
                  pl.BlockSpec(memory_space=pl.ANY), pl.BlockSpec(memory_space=pl.ANY),
                  pl.BlockSpec(memory_space=pl.ANY)],
        out_specs=per_b((t, kv_heads * group * head_dim)),
        scratch_shapes=[pltpu.VMEM((2, cache_idx_t.shape[1], past), F32),
                        pltpu.VMEM((2, kvw, past), F32),
                        pltpu.VMEM((2, kvw, past), F32),
                        pltpu.SemaphoreType.DMA((2, 3)),
                        pltpu.VMEM((t, past + cw), I32),
                        pltpu.VMEM((1, rows, LANES), F32),
                        pltpu.VMEM((1, rows, LANES), F32),
                        pltpu.VMEM((1, rows, kvw), F32)])
    return pl.pallas_call(
        kern,
        grid_spec=grid_spec,
        out_shape=jax.ShapeDtypeStruct((db, t, kv_heads * group * head_dim), F32),
        compiler_params=_params(),
        name="sparse_attn_sample",
    )(page_table, qi, wi, qbd, ki_new, ka_new, va_new, tri, cache_idx_t, cache_k_t, cache_v_t)


def _log_gamma(heads):
    return np.log(1.0 - np.exp2(-RET_DECAY_OFFSET - np.arange(heads, dtype=np.float64)))


def _retention_tables(heads, chunk, dk, group):
    lg = _log_gamma(heads)
    i = np.arange(chunk, dtype=np.float64)
    diff = i[:, None] - i[None, :]
    decay = np.where(diff >= 0, np.exp(lg[:, None, None] * np.maximum(diff, 0.0)), 0.0)
    decay_bd = np.zeros((heads, group * chunk, group * chunk))
    for gidx in range(group):
        decay_bd[:, gidx * chunk:(gidx + 1) * chunk, gidx * chunk:(gidx + 1) * chunk] = decay
    q_scale = np.tile(np.exp(lg[:, None] * (i + 1.0)), (1, group))
    k_scale = np.tile(np.exp(lg[:, None] * (chunk - 1.0 - i)), (1, group))
    state_scale = np.exp(lg * chunk)
    rows_q = np.broadcast_to(q_scale[:, :, None], (heads, group * chunk, dk))
    rows_k = np.broadcast_to(k_scale[:, :, None], (heads, group * chunk, dk))
    cols_k = np.broadcast_to(k_scale[:, None, :], (heads, dk, group * chunk))
    as32 = lambda a: jnp.asarray(np.ascontiguousarray(a), F32)
    return as32(decay_bd), as32(rows_q), as32(rows_k), as32(cols_k), [float(v) for v in state_scale]


def _groupnorm_gate(o, gn_g, g_r):
    mu = jnp.mean(o, axis=-1, keepdims=True)
    d = o - mu
    var = jnp.mean(d * d, axis=-1, keepdims=True)
    return (g_r * _sigmoid(g_r)) * ((d * lax.rsqrt(var + EPS)) * gn_g)


def _retention_prompt_kernel(q_ref, k_ref, v_ref, g_ref, gn_ref, dec_ref, qs_ref, ks_ref, o_ref, st_ref, state,
                             *, heads, dk, dv, state_scale):
    c = pl.program_id(1)

    @pl.when(c == 0)
    def _():
        state[...] = jnp.zeros(state.shape, F32)

    for h in range(heads):
        q = q_ref[0, :, h * dk:(h + 1) * dk]
        k = k_ref[0, :, h * dk:(h + 1) * dk]
        v = v_ref[0, :, h * dv:(h + 1) * dv].astype(BF16)
        s_old = state[h]
        inner = _dot_nt(q.astype(BF16), k.astype(BF16)) * dec_ref[h]
        o = _dot(inner.astype(BF16), v) + _dot((q * qs_ref[h]).astype(BF16), s_old.astype(BF16))
        state[h] = state_scale[h] * s_old + _dot((k * ks_ref[h]).T.astype(BF16), v)
        o_ref[0, :, h * dv:(h + 1) * dv] = _groupnorm_gate(o, gn_ref[:, h * dv:(h + 1) * dv],
                                                           g_ref[0, :, h * dv:(h + 1) * dv])

    @pl.when(c == pl.num_programs(1) - 1)
    def _():
        st_ref[0] = state[...]


def _retention_prompt(q, k, v, g, gn_g, heads, dk, dv):
    b, s, _ = q.shape
    chunk = min(RET_CHUNK, s)
    dec, qs, ks, _, state_scale = _retention_tables(heads, chunk, dk, 1)
    blk = lambda w: pl.BlockSpec((1, chunk, w), lambda bi, ci: (bi, ci, 0))
    const = lambda a: pl.BlockSpec(a.shape, lambda bi, ci: (0,) * a.ndim)
    gn2 = gn_g.reshape(1, heads * dv)
    kern = functools.partial(_retention_prompt_kernel, heads=heads, dk=dk, dv=dv, state_scale=state_scale)
    return pl.pallas_call(
        kern,
        grid=(b, s // chunk),
        in_specs=[blk(heads * dk), blk(heads * dk), blk(heads * dv), blk(heads * dv),
                  const(gn2), const(dec), const(qs), const(ks)],
        out_specs=[blk(heads * dv), pl.BlockSpec((1, heads, dk, dv), lambda bi, ci: (bi, 0, 0, 0))],
        out_shape=[jax.ShapeDtypeStruct((b, s, heads * dv), F32),
                   jax.ShapeDtypeStruct((b, heads, dk, dv), F32)],
        scratch_shapes=[pltpu.VMEM((heads, dk, dv), F32)],
        compiler_params=_params(2),
        name="retention_prompt",
    )(q, k, v, g, gn2, dec, qs, ks)


def _retention_sample_kernel(q_ref, k_ref, kt_ref, v_ref, g_ref, gn_ref, dec_ref, qs_ref, kst_ref, st_ref,
                             o_ref, sto_ref, *, heads, dk, dv, t, group, state_scale):
    rows = group * t
    col_batch = lax.broadcasted_iota(I32, (dk, rows), 1) // t
    for h in range(heads):
        q = q_ref[:, h * dk:(h + 1) * dk]
        k = k_ref[:, h * dk:(h + 1) * dk]
        v = v_ref[:, h * dv:(h + 1) * dv].astype(BF16)
        inner = _dot_nt(q.astype(BF16), k.astype(BF16)) * dec_ref[h]
        o = _dot(inner.astype(BF16), v)
        qd = (q * qs_ref[h]).astype(BF16)
        kdt = kt_ref[0, h] * kst_ref[h]
        cross = []
        for bi in range(group):
            s_old = st_ref[bi, h]
            cross.append(_dot(qd, s_old.astype(BF16))[bi * t:(bi + 1) * t])
            kd_b = jnp.where(col_batch == bi, kdt, 0.0).astype(BF16)
            sto_ref[bi, h] = state_scale[h] * s_old + _dot(kd_b, v)
        o = o + jnp.concatenate(cross, axis=0)
        o_ref[:, h * dv:(h + 1) * dv] = _groupnorm_gate(o, gn_ref[:, h * dv:(h + 1) * dv],
                                                        g_ref[:, h * dv:(h + 1) * dv])


def _retention_sample(state, q, k, v, g, gn_g, heads, dk, dv, t):
    db = state.shape[0]
    group = min(SAMPLE_GROUP, db)
    rows = group * t
    dec, qs, _, kst, state_scale = _retention_tables(heads, t, dk, group)
    kt = k.reshape(db // group, rows, heads, dk).transpose(0, 2, 3, 1)
    gn2 = gn_g.reshape(1, heads * dv)
    blk = lambda w: pl.BlockSpec((rows, w), lambda i: (i, 0))
    const = lambda a: pl.BlockSpec(a.shape, lambda i: (0,) * a.ndim)
    st_spec = pl.BlockSpec((group, heads, dk, dv), lambda i: (i, 0, 0, 0))
    kern = functools.partial(_retention_sample_kernel, heads=heads, dk=dk, dv=dv, t=t, group=group,
                             state_scale=state_scale)
    return pl.pallas_call(
        kern,
        grid=(db // group,),
        in_specs=[blk(heads * dk), blk(heads * dk),
                  pl.BlockSpec((1, heads, dk, rows), lambda i: (i, 0, 0, 0)),
                  blk(heads * dv), blk(heads * dv), const(gn2), const(dec), const(qs), const(kst), st_spec],
        out_specs=[blk(heads * dv), st_spec],
        out_shape=[jax.ShapeDtypeStruct((db * t, heads * dv), F32),
                   jax.ShapeDtypeStruct(state.shape, F32)],
        compiler_params=_params(),
        name="retention_sample",
    )(q, k, kt, v, g, gn2, dec, qs, kst, state)


def _mem_kv_kernel(x_ref, g_ref, w_ref, k_ref, v_ref):
    hb = _rmsnorm(x_ref[...], g_ref[...]).astype(BF16)
    half = w_ref.shape[1] // 2
    k_ref[...] = _dot(hb, w_ref[:, :half])
    v_ref[...] = _dot(hb, w_ref[:, half:])


def _memory_kv(mem2d, g, w_bf16):
    t, d = mem2d.shape
    half = w_bf16.shape[1] // 2
    tm = min(PROJ_ROWS, t)
    return pl.pallas_call(
        _mem_kv_kernel,
        grid=(t // tm,),
        in_specs=[pl.BlockSpec((tm, d), lambda i: (i, 0)), pl.BlockSpec((1, d), lambda i: (0, 0)), _whole_vmem()],
        out_specs=[pl.BlockSpec((tm, half), lambda i: (i, 0))] * 2,
        out_shape=[jax.ShapeDtypeStruct((t, half), F32)] * 2,
        compiler_params=_params(),
        name="memory_kv",
    )(mem2d, g.reshape(1, d), w_bf16)


def _mem_head_attention(q, k, v):
    s = _dot_nt(q, k)
    p = jnp.exp(s - jnp.max(s, axis=1, keepdims=True))
    return _dot(p.astype(BF16), v) * (1.0 / jnp.sum(p, axis=1, keepdims=True))


def _mem_attn_prompt_kernel(q_ref, k_ref, v_ref, o_ref, *, heads, dim):
    for h in range(heads):
        sl = slice(h * dim, (h + 1) * dim)
        o_ref[0, :, sl] = _mem_head_attention(q_ref[0, :, sl], k_ref[0, :, sl], v_ref[0, :, sl])


def _memory_attention_prompt(q, mk, mv, heads, dim):
    b, s, w = q.shape
    tm = min(MEM_ROWS, s)
    m = mk.shape[1]
    return pl.pallas_call(
        functools.partial(_mem_attn_prompt_kernel, heads=heads, dim=dim),
        grid=(b, s // tm),
        in_specs=[pl.BlockSpec((1, tm, w), lambda bi, i: (bi, i, 0)),
                  pl.BlockSpec((1, m, w), lambda bi, i: (bi, 0, 0)),
                  pl.BlockSpec((1, m, w), lambda bi, i: (bi, 0, 0))],
        out_specs=pl.BlockSpec((1, tm, w), lambda bi, i: (bi, i, 0)),
        out_shape=jax.ShapeDtypeStruct((b, s, w), F32),
        compiler_params=_params(2),
        name="memory_attn_prompt",
    )(q, mk, mv)


def _mem_attn_sample_kernel(q_ref, k_ref, v_ref, o_ref, *, heads, dim, t, group):
    for bi in range(group):
        for h in range(heads):
            sl = slice(h * dim, (h + 1) * dim)
            o = _mem_head_attention(q_ref[:, sl].astype(BF16), k_ref[bi, :, h, :].astype(BF16),
                                    v_ref[bi, :, h, :].astype(BF16))
            o_ref[bi * t:(bi + 1) * t, sl] = o[bi * t:(bi + 1) * t]


def _memory_attention_sample(q, mk, mv, t):
    db, m, heads, dim = mk.shape
    group = min(SAMPLE_GROUP, db)
    w = heads * dim
    mem_spec = pl.BlockSpec((group, m, heads, dim), lambda i: (i, 0, 0, 0))
    return pl.pallas_call(
        functools.partial(_mem_attn_sample_kernel, heads=heads, dim=dim, t=t, group=group),
        grid=(db // group,),
        in_specs=[pl.BlockSpec((group * t, w), lambda i: (i, 0)), mem_spec, mem_spec],
        out_specs=pl.BlockSpec((group * t, w), lambda i: (i, 0)),
        out_shape=jax.ShapeDtypeStruct((db * t, w), F32),
        compiler_params=_params(),
        name="memory_attn_sample",
    )(q, mk, mv)


def _merge_ffn_kernel(x_ref, gt_ref, oa_ref, or_ref, om_ref, wo_ref, fg_ref, wg_ref, wu_ref, wd_ref, fin_ref,
                      y_ref, *, final_norm):
    d = x_ref.shape[1]
    u = (gt_ref[:, :d] * oa_ref[...] + gt_ref[:, d:2 * d] * or_ref[...] + gt_ref[:, 2 * d:] * om_ref[...])
    x1 = x_ref[...] + _dot(u.astype(BF16), wo_ref[...])
    hb = _rmsnorm(x1, fg_ref[...]).astype(BF16)
    a = _dot(hb, wg_ref[...])
    act = (a * _sigmoid(a)) * _dot(hb, wu_ref[...])
    x2 = x1 + _dot(act.astype(BF16), wd_ref[...])
    y_ref[...] = _rmsnorm(x2, fin_ref[...]) if final_norm else x2


def _merge_ffn(x2d, gates, o_a, o_r, o_m, w_out, ffn_g, w_gate, w_up, w_down, final_g, final_norm):
    t, d = x2d.shape
    tm = min(PROJ_ROWS, t)
    row = lambda w: pl.BlockSpec((tm, w), lambda i: (i, 0))
    vec = pl.BlockSpec((1, d), lambda i: (0, 0))
    return pl.pallas_call(
        functools.partial(_merge_ffn_kernel, final_norm=final_norm),
        grid=(t // tm,),
        in_specs=[row(d), row(gates.shape[1]), row(d), row(d), row(d), _whole_vmem(), vec,
                  _whole_vmem(), _whole_vmem(), _whole_vmem(), vec],
        out_specs=row(d),
        out_shape=jax.ShapeDtypeStruct((t, d), F32),
        compiler_params=_params(),
        name="merge_ffn",
    )(x2d, gates, o_a, o_r, o_m, w_out, ffn_g.reshape(1, d), w_gate, w_up, w_down, final_g.reshape(1, d))


def kernel(x_prompt, x_sample, cache_k, cache_v, cache_idx_k, state_ret, cache_mem_k, cache_mem_v, page_table,
           mem_prompt, attn_norm_g, w_in, ret_gn_g, w_out, ffn_norm_g, w_gate, w_up, w_down, mem_norm_g, w_mem_kv,
           final_norm_g):
    depth = w_in.shape[0]
    b, s, d = x_prompt.shape
    db, t, _ = x_sample.shape
    _, n_pool, page, kv_heads, head_dim = cache_k.shape
    idx_dim = cache_idx_k.shape[-1]
    _, _, ret_heads, ret_dk, ret_dv = state_ret.shape
    _, _, mem_tokens, mem_heads, mem_dim = cache_mem_k.shape
    q_heads = d // head_dim
    group = q_heads // kv_heads
    kvw = kv_heads * head_dim
    n_pages = page_table.shape[1]
    past = n_pages * page
    dims = dict(idx_dim=idx_dim, ret_dk=ret_dk, head_dim=head_dim, mem_dim=mem_dim)
    lay, _ = _proj_layout(d, kvw, idx_dim, ret_heads * ret_dk, ret_heads * ret_dv, mem_heads * mem_dim,
                          w_in.shape[2] - (d + 2 * kvw + IDX_HEADS * idx_dim + idx_dim + IDX_HEADS
                                           + 2 * ret_heads * ret_dk + 2 * ret_heads * ret_dv + mem_heads * mem_dim))
    o_ki = d + 2 * kvw + IDX_HEADS * idx_dim
    o_qr = o_ki + idx_dim + IDX_HEADS

    cos_p, sin_p = _rotary_tables(jnp.arange(s), ret_dk, min(PROJ_ROWS, b * s))
    cos_s, sin_s = _rotary_tables(past + jnp.arange(t), ret_dk, min(PROJ_ROWS, db * t))
    tri = (jnp.arange(TRI_W)[:, None] <= jnp.arange(TRI_W)[None, :]).astype(BF16)
    topk_p = min(TOPK_MAX, s // 4)
    topk_s = min(TOPK_MAX, (past + t) // 4)

    xp = x_prompt.reshape(b * s, d)
    xs = x_sample.reshape(db * t, d)
    outs = {k: [] for k in ("k_p", "v_p", "ik_p", "rs_p", "mk_p", "mv_p", "k_s", "v_s", "ik_s", "rs_s")}
    for l in range(depth):
        last = l == depth - 1
        wl = w_in[l]
        w_packed = jnp.concatenate(
            [wl[:, :o_ki], wl[:, o_ki:o_qr], jnp.zeros((d, LANES - (o_qr - o_ki)), wl.dtype), wl[:, o_qr:]],
            axis=1).astype(BF16)
        w_out_b, w_gate_b, w_up_b, w_down_b = (a[l].astype(BF16) for a in (w_out, w_gate, w_up, w_down))

        p = _project(xp, attn_norm_g[l], w_packed, cos_p, sin_p, lay, dims, BF16)
        qi = p["qi"].reshape(b * s, IDX_HEADS, idx_dim).transpose(1, 0, 2)
        qa = p["qa"].reshape(b * s, q_heads, head_dim).transpose(1, 0, 2)
        kit = p["ki"].astype(BF16).reshape(b, s, idx_dim).transpose(0, 2, 1)
        kt = p["ka"].astype(BF16).reshape(b, s, kvw).transpose(0, 2, 1)
        vb = p["va"].astype(BF16).reshape(b, s, kvw)
        o_a = _sparse_attention_prompt(qi, p["wi"], qa, kit, kt, vb, tri, topk_p, kv_heads, head_dim)
        o_r, st_p = _retention_prompt(p["qr"].reshape(b, s, -1), p["kr"].reshape(b, s, -1),
                                      p["vr"].reshape(b, s, -1), p["gr"].reshape(b, s, -1), ret_gn_g[l],
                                      ret_heads, ret_dk, ret_dv)
        mk, mv = _memory_kv(mem_prompt.reshape(b * mem_tokens, d), mem_norm_g[l], w_mem_kv[l].astype(BF16))
        o_m = _memory_attention_prompt(p["qm"].reshape(b, s, -1), mk.astype(BF16).reshape(b, mem_tokens, -1),
                                       mv.astype(BF16).reshape(b, mem_tokens, -1), mem_heads, mem_dim)
        xp = _merge_ffn(xp, p["gt"], o_a, o_r.reshape(b * s, -1), o_m.reshape(b * s, -1), w_out_b, ffn_norm_g[l],
                        w_gate_b, w_up_b, w_down_b, final_norm_g, last)
        outs["k_p"].append(p["ka"].reshape(b, s, kv_heads, head_dim))
        outs["v_p"].append(p["va"].reshape(b, s, kv_heads, head_dim))
        outs["ik_p"].append(p["ki"].reshape(b, s, idx_dim))
        outs["rs_p"].append(st_p)
        outs["mk_p"].append(mk.reshape(b, mem_tokens, mem_heads, mem_dim))
        outs["mv_p"].append(mv.reshape(b, mem_tokens, mem_heads, mem_dim))

        q = _project(xs, attn_norm_g[l], w_packed, cos_s, sin_s, lay, dims, F32)
        qi_s = q["qi"].reshape(db, t, IDX_HEADS, idx_dim).transpose(0, 2, 1, 3).reshape(db, IDX_HEADS * t, idx_dim)
        qa_s = q["qa"].reshape(db, t, kv_heads, group, head_dim).transpose(0, 2, 3, 1, 4)
        qbd = (qa_s.reshape(db, kv_heads, group * t, 1, head_dim)
               * jnp.eye(kv_heads, dtype=F32)[None, :, None, :, None]).reshape(db, kv_heads * group * t, kvw)
        pad_rows = lambda a: jnp.pad(a.reshape(db, t, -1), ((0, 0), (0, LANES - t), (0, 0))).astype(BF16)
        o_a = _sparse_attention_sample(
            page_table, qi_s.astype(BF16), q["wi"].reshape(db, t, IDX_HEADS), qbd.astype(BF16),
            pad_rows(q["ki"]), pad_rows(q["ka"]), pad_rows(q["va"]), tri,
            jnp.swapaxes(cache_idx_k[l], 1, 2),
            cache_k[l].transpose(0, 2, 3, 1).reshape(n_pool, kvw, page),
            cache_v[l].transpose(0, 2, 3, 1).reshape(n_pool, kvw, page),
            topk_s, kv_heads, group, head_dim).reshape(db * t, d)
        o_r, st_s = _retention_sample(state_ret[l], q["qr"], q["kr"], q["vr"], q["gr"], ret_gn_g[l],
                                      ret_heads, ret_dk, ret_dv, t)
        o_m = _memory_attention_sample(q["qm"], cache_mem_k[l], cache_mem_v[l], t)
        xs = _merge_ffn(xs, q["gt"], o_a, o_r, o_m, w_out_b, ffn_norm_g[l], w_gate_b, w_up_b, w_down_b,
                        final_norm_g, last)
        outs["k_s"].append(q["ka"].reshape(db, t, kv_heads, head_dim))
        outs["v_s"].append(q["va"].reshape(db, t, kv_heads, head_dim))
        outs["ik_s"].append(q["ki"].reshape(db, t, idx_dim))
        outs["rs_s"].append(st_s)

    stk = lambda k: jnp.stack(outs[k])
    return (xp.reshape(b, s, d), xs.reshape(db, t, d), stk("k_p"), stk("v_p"), stk("ik_p"), stk("rs_p"),
            stk("mk_p"), stk("mv_p"), stk("k_s"), stk("v_s"), stk("ik_s"), stk("rs_s"))
```

```python
import functools
import math

import numpy as np
import jax
import jax.numpy as jnp
from jax import lax
from jax.experimental import pallas as pl
from jax.experimental.pallas import tpu as pltpu

F32 = jnp.float32
BF16 = jnp.bfloat16
I32 = jnp.int32

IDX_HEADS = 4
TOPK_MAX = 256
RET_CHUNK = 128
RET_DECAY_OFFSET = 5.0
ROPE_BASE = 10000.0
EPS = 1e-6

LANES = 128
VMEM_LIMIT_BYTES = 56 * 1024 * 1024

NEG = -1e30
INT_MIN = -(2 ** 31)
LOG2E = math.log2(math.e)
MIN_NORMAL = 2.0 ** -126

PROJ_ROWS = 256
ATTN_TQ = 128
ATTN_KB = 1024
SAMPLE_KB = 1024
COUNT_ELEMS = 65536
TRI_W = 256
MEM_ROWS = 512
SAMPLE_GROUP = 8


def _params(n_axes=1):
    return pltpu.CompilerParams(dimension_semantics=("arbitrary",) * n_axes,
                                vmem_limit_bytes=VMEM_LIMIT_BYTES)


def _whole_vmem():
    return pl.BlockSpec(memory_space=pltpu.VMEM)


def _rmsnorm(x, g):
    return (x * lax.rsqrt(jnp.mean(x * x, axis=-1, keepdims=True) + EPS)) * g


def _sigmoid(x):
    return 1.0 / (1.0 + jnp.exp(-x))


def _dot(a, b):
    return jnp.dot(a, b, preferred_element_type=F32)


def _dot_nt(a, b):
    return lax.dot_general(a, b, (((1,), (1,)), ((), ())), preferred_element_type=F32)


def _proj_layout(d_model, kv_w, idx_dim, ret_qk_w, ret_v_w, mem_w, n_gate):
    segs = [("qa", d_model), ("ka", kv_w), ("va", kv_w), ("qi", IDX_HEADS * idx_dim), ("kw", LANES),
            ("qr", ret_qk_w), ("kr", ret_qk_w), ("vr", ret_v_w), ("gr", ret_v_w), ("qm", mem_w),
            ("gt", n_gate)]
    out, off = {}, 0
    for name, w in segs:
        out[name] = (off, off + w)
        off += w
    return out, off


def _proj_kernel(x_ref, g_ref, w_ref, cos_ref, sin_ref,
                 qa_ref, ka_ref, va_ref, qi_ref, ki_ref, wi_ref, qr_ref, kr_ref, vr_ref, gr_ref, qm_ref, gt_ref,
                 *, lay, idx_dim, ret_dk, qa_scale, wi_scale, kr_scale, qm_scale):
    hb = _rmsnorm(x_ref[...], g_ref[...]).astype(BF16)

    def seg(name):
        lo, hi = lay[name]
        return _dot(hb, w_ref[:, lo:hi])

    qa_ref[...] = (seg("qa") * qa_scale).astype(qa_ref.dtype)
    ka_ref[...] = seg("ka")
    va_ref[...] = seg("va")
    qi_ref[...] = seg("qi").astype(qi_ref.dtype)
    kw = seg("kw")
    ki_ref[...] = kw[:, :idx_dim]
    wi_ref[...] = kw[:, idx_dim:idx_dim + IDX_HEADS] * wi_scale

    cos2 = cos_ref[...]
    sin2 = sin_ref[...]

    def rotary(z, out_ref, scale):
        for h in range(z.shape[1] // ret_dk):
            zh = z[:, h * ret_dk:(h + 1) * ret_dk]
            r = zh * cos2 + pltpu.roll(zh, ret_dk // 2, 1) * sin2
            out_ref[:, h * ret_dk:(h + 1) * ret_dk] = r if scale is None else r * scale

    rotary(seg("qr"), qr_ref, None)
    rotary(seg("kr"), kr_ref, kr_scale)
    vr_ref[...] = seg("vr")
    gr_ref[...] = seg("gr")
    qm_ref[...] = (seg("qm") * qm_scale).astype(qm_ref.dtype)
    gt_ref[...] = _sigmoid(seg("gt"))


def _project(x2d, g, w_packed, cos2, sin2, lay, dims, q_dtype):
    t, d = x2d.shape
    tm = min(PROJ_ROWS, t)
    n_pos_blocks = cos2.shape[0] // tm
    widths = {k: hi - lo for k, (lo, hi) in lay.items()}
    out_defs = [("qa", widths["qa"], q_dtype), ("ka", widths["ka"], F32), ("va", widths["va"], F32),
                ("qi", widths["qi"], q_dtype), ("ki", dims["idx_dim"], F32), ("wi", IDX_HEADS, F32),
                ("qr", widths["qr"], F32), ("kr", widths["kr"], F32), ("vr", widths["vr"], F32),
                ("gr", widths["gr"], F32), ("qm", widths["qm"], q_dtype), ("gt", widths["gt"], F32)]
    row = lambda w: pl.BlockSpec((tm, w), lambda i: (i, 0))
    kern = functools.partial(
        _proj_kernel, lay=lay, idx_dim=dims["idx_dim"], ret_dk=dims["ret_dk"],
        qa_scale=dims["head_dim"] ** -0.5 * LOG2E, wi_scale=(IDX_HEADS ** -0.5) * (dims["idx_dim"] ** -0.5),
        kr_scale=dims["ret_dk"] ** -0.5, qm_scale=dims["mem_dim"] ** -0.5)
    outs = pl.pallas_call(
        kern,
        grid=(t // tm,),
        in_specs=[row(d), pl.BlockSpec((1, d), lambda i: (0, 0)), _whole_vmem(),
                  pl.BlockSpec((tm, LANES), lambda i: (i % n_pos_blocks, 0)),
                  pl.BlockSpec((tm, LANES), lambda i: (i % n_pos_blocks, 0))],
        out_specs=[row(w) for _, w, _ in out_defs],
        out_shape=[jax.ShapeDtypeStruct((t, w), dt) for _, w, dt in out_defs],
        compiler_params=_params(),
        name="proj",
    )(x2d, g.reshape(1, d), w_packed, cos2, sin2)
    return {name: o for (name, _, _), o in zip(out_defs, outs)}


def _rotary_tables(pos, ret_dk, rows):
    half = ret_dk // 2
    inv = ROPE_BASE ** (-jnp.arange(half, dtype=F32) / half)
    ang = pos.astype(F32)[:, None] * inv[None, :]
    cos, sin = jnp.cos(ang), jnp.sin(ang)
    cos2 = jnp.concatenate([cos, cos], axis=1)
    sin2 = jnp.concatenate([-sin, sin], axis=1)
    reps = max(1, rows // cos2.shape[0])
    return jnp.tile(cos2, (reps, 1)), jnp.tile(sin2, (reps, 1))


def _index_scores(s4, w):
    acc = w[:, 0:1] * jnp.maximum(s4[0], 0.0)
    for h in range(1, IDX_HEADS):
        acc = acc + w[:, h:h + 1] * jnp.maximum(s4[h], 0.0)
    return jnp.where(jnp.abs(acc) < MIN_NORMAL, 0.0, acc)


def _sortable_key(bits):
    return jnp.where(bits >= 0, bits, bits ^ 0x7FFFFFFF)


def _count_chunk(rows):
    return max(LANES, COUNT_ELEMS // rows)


def _topk_threshold(keys_ref, rows, n_chunks, topk):
    cw = _count_chunk(rows)

    def tree_sum(parts):
        while len(parts) > 1:
            parts = [a + b for a, b in zip(parts[::2], parts[1::2])] + parts[len(parts) & ~1:]
        return parts[0]

    def count_ge(cand):
        cb = jnp.broadcast_to(cand, (rows, LANES))

        def body(c, a):
            blk = keys_ref[:, pl.ds(pl.multiple_of(c * cw, cw), cw)]
            return a + tree_sum([jnp.where(blk[:, u * LANES:(u + 1) * LANES] >= cb, 1.0, 0.0)
                                 for u in range(cw // LANES)])

        a = lax.fori_loop(0, n_chunks, body, jnp.zeros((rows, LANES), F32))
        return jnp.sum(a, axis=1, keepdims=True)

    def bit_step(b, u):
        cand_u = u | lax.shift_left(jnp.int32(1), 31 - b)
        return jnp.where(count_ge(cand_u ^ INT_MIN) >= topk, cand_u, u)

    u = lax.fori_loop(0, 32, bit_step, jnp.zeros((rows, 1), I32))
    thr = u ^ INT_MIN
    need = jnp.where(thr == INT_MIN, 0.0, topk - count_ge(thr + 1))
    return thr, need


def _selection_bias(keys, thr, need, run, tri):
    tw = tri.shape[0]
    eq = keys == thr
    gt = keys > thr
    eq_b = jnp.where(eq, 1.0, 0.0).astype(BF16)
    parts = []
    for c in range(keys.shape[1] // tw):
        pre = _dot(eq_b[:, c * tw:(c + 1) * tw], tri) + run
        run = pre[:, tw - 1:]
        parts.append(pre)
    prefix = parts[0] if len(parts) == 1 else jnp.concatenate(parts, axis=1)
    bias = jnp.where(gt, 0.0, jnp.where(eq, jnp.where(prefix <= need, 0.0, NEG), NEG))
    return bias, run


def _lane_tile(x, width):
    reps = width // x.shape[1]
    return x if reps == 1 else jnp.concatenate([x] * reps, axis=1)


def _softmax_tile(s, pv, m_ref, l_ref, acc_ref, idx):
    m_prev = m_ref[idx]
    m_new = jnp.maximum(m_prev, jnp.max(s, axis=1, keepdims=True))
    alpha = jnp.exp2(m_prev - m_new)
    p = jnp.exp2(s - _lane_tile(m_new, s.shape[1]))
    if l_ref is not None:
        l_ref[idx] = alpha * l_ref[idx] + jnp.sum(p, axis=1, keepdims=True)
    acc = acc_ref[idx]
    acc_ref[idx] = _lane_tile(alpha, acc.shape[1]) * acc + pv(p.astype(BF16))
    m_ref[idx] = m_new


def _sparse_prompt_kernel(qi_ref, wi_ref, qa_ref, kit_ref, kt_ref, v_ref, tri_ref, o_ref,
                          keys_ref, m_ref, acc_ref, *, tq, kb, topk, kv_heads, group, head_dim):
    i = pl.program_id(1)
    n_tiles = ((i + 1) * tq + kb - 1) // kb
    row_pos = i * tq + lax.broadcasted_iota(I32, (tq, kb), 0)
    col_iota = lax.broadcasted_iota(I32, (tq, kb), 1)
    w = wi_ref[...]
    qi = qi_ref[...].reshape(IDX_HEADS * tq, qi_ref.shape[2])

    def score_tile(j, c):
        k0 = pl.multiple_of(j * kb, kb)
        s4 = _dot(qi, kit_ref[0, :, pl.ds(k0, kb)]).reshape(IDX_HEADS, tq, kb)
        key = _sortable_key(lax.bitcast_convert_type(_index_scores(s4, w), I32))
        keys_ref[:, pl.ds(k0, kb)] = jnp.where(k0 + col_iota <= row_pos, key, INT_MIN)
        return c

    lax.fori_loop(0, n_tiles, score_tile, 0)

    cw = _count_chunk(tq)
    thr, need = _topk_threshold(keys_ref, tq, n_tiles * (kb // cw), topk)

    m_ref[...] = jnp.full(m_ref.shape, NEG, F32)
    acc_ref[...] = jnp.zeros(acc_ref.shape, F32)

    def attn_tile(j, run):
        k0 = pl.multiple_of(j * kb, kb)
        bias, run = _selection_bias(keys_ref[:, pl.ds(k0, kb)], thr, need, run, tri_ref[...])
        for n in range(kv_heads):
            qn = qa_ref[n * group:(n + 1) * group].reshape(group * tq, head_dim)
            s = _dot(qn, kt_ref[0, n * head_dim:(n + 1) * head_dim, pl.ds(k0, kb)])
            s = (s.reshape(group, tq, kb) + bias[None]).reshape(group * tq, kb)
            vn = v_ref[0, pl.ds(k0, kb), n * LANES:(n + 1) * LANES]
            _softmax_tile(s, lambda p: _dot(p, vn), m_ref, None, acc_ref, n)
        return run

    lax.fori_loop(0, n_tiles, attn_tile, jnp.zeros((tq, 1), F32))

    for n in range(kv_heads):
        acc = acc_ref[n]
        o = acc * pltpu.roll(1.0 / acc, LANES - head_dim, 1)
        for g in range(group):
            h = n * group + g
            o_ref[:, h * head_dim:(h + 1) * head_dim] = o[g * tq:(g + 1) * tq, :head_dim]


def _sparse_attention_prompt(qi, wi, qa, kit, kt, v, tri, topk, kv_heads, head_dim):
    q_heads = qa.shape[0]
    b, s, _ = v.shape
    tq, kb = min(ATTN_TQ, s), min(ATTN_KB, s)
    nq = s // tq
    group = q_heads // kv_heads
    kern = functools.partial(_sparse_prompt_kernel, tq=tq, kb=kb, topk=topk, kv_heads=kv_heads,
                             group=group, head_dim=head_dim)
    resident = lambda shape: pl.BlockSpec((1,) + shape, lambda bi, i: (bi, 0, 0), pipeline_mode=pl.Buffered(1))
    return pl.pallas_call(
        kern,
        grid=(b, nq),
        in_specs=[pl.BlockSpec((IDX_HEADS, tq, qi.shape[2]), lambda bi, i: (0, bi * nq + i, 0)),
                  pl.BlockSpec((tq, IDX_HEADS), lambda bi, i: (bi * nq + i, 0)),
                  pl.BlockSpec((q_heads, tq, head_dim), lambda bi, i: (0, bi * nq + i, 0)),
                  resident(kit.shape[1:]), resident(kt.shape[1:]), resident(v.shape[1:]), _whole_vmem()],
        out_specs=pl.BlockSpec((tq, q_heads * head_dim), lambda bi, i: (bi * nq + i, 0)),
        out_shape=jax.ShapeDtypeStruct((b * s, q_heads * head_dim), F32),
        scratch_shapes=[pltpu.VMEM((tq, s), I32),
                        pltpu.VMEM((kv_heads, group * tq, LANES), F32),
                        pltpu.VMEM((kv_heads, group * tq, LANES), F32)],
        compiler_params=_params(2),
        name="sparse_attn_prompt",
    )(qi, wi, qa, kit, kt, v, tri)


def _sparse_sample_kernel(pt_ref, qi_ref, wi_ref, qbd_ref, kin_ref, kan_ref, van_ref, tri_ref,
                          cidx_hbm, ck_hbm, cv_hbm, o_ref,
                          ibuf, kbuf, vbuf, sems, keys_ref, m_ref, l_ref, acc_ref,
                          *, t, kb, topk, n_pages, page, kv_heads, group, head_dim):
    b = pl.program_id(0)
    nb = pl.num_programs(0)
    past = n_pages * page
    rows = kv_heads * group * t
    cw = _count_chunk(t)

    def page_copies(bb, slot, p):
        pg = pt_ref[bb, p]
        dst = pl.ds(pl.multiple_of(p * page, page), page)
        return (pltpu.make_async_copy(cidx_hbm.at[pg], ibuf.at[slot, :, dst], sems.at[slot, 0]),
                pltpu.make_async_copy(ck_hbm.at[pg], kbuf.at[slot, :, dst], sems.at[slot, 1]),
                pltpu.make_async_copy(cv_hbm.at[pg], vbuf.at[slot, :, dst], sems.at[slot, 2]))

    def start_fetch(bb, slot):
        def body(p, c):
            for cp in page_copies(bb, slot, p):
                cp.start()
            return c
        lax.fori_loop(0, n_pages, body, 0)

    def wait_fetch(bb, slot):
        def body(p, c):
            for cp in page_copies(bb, slot, p):
                cp.wait()
            return c
        lax.fori_loop(0, n_pages, body, 0)

    slot = b % 2

    @pl.when(b == 0)
    def _():
        start_fetch(b, slot)

    @pl.when(b + 1 < nb)
    def _():
        start_fetch(b + 1, 1 - slot)

    wait_fetch(b, slot)

    w = wi_ref[0]
    qi = qi_ref[0]
    n_tiles = past // kb

    def score_tile(j, c):
        k0 = pl.multiple_of(j * kb, kb)
        s4 = _dot(qi, ibuf[slot, :, pl.ds(k0, kb)].astype(BF16)).reshape(IDX_HEADS, t, kb)
        keys_ref[:, pl.ds(k0, kb)] = _sortable_key(lax.bitcast_convert_type(_index_scores(s4, w), I32))
        return c

    lax.fori_loop(0, n_tiles, score_tile, 0)
    s4 = _dot_nt(qi, kin_ref[0]).reshape(IDX_HEADS, t, LANES)
    key_new = _sortable_key(lax.bitcast_convert_type(_index_scores(s4, w), I32))
    new_col = lax.broadcasted_iota(I32, (t, LANES), 1)
    new_row = lax.broadcasted_iota(I32, (t, LANES), 0)
    keys_ref[:, past:past + LANES] = jnp.where(new_col <= new_row, key_new, INT_MIN)
    keys_ref[:, past + LANES:past + cw] = jnp.full((t, cw - LANES), INT_MIN, I32)

    thr, need = _topk_threshold(keys_ref, t, past // cw + 1, topk)

    m_ref[...] = jnp.full(m_ref.shape, NEG, F32)
    l_ref[...] = jnp.zeros(l_ref.shape, F32)
    acc_ref[...] = jnp.zeros(acc_ref.shape, F32)
    qbd = qbd_ref[0]

    def tile_bias(bias, width):
        return jnp.broadcast_to(bias[None], (rows // t, t, width)).reshape(rows, width)

    def attn_tile(j, run):
        k0 = pl.multiple_of(j * kb, kb)
        bias, run = _selection_bias(keys_ref[:, pl.ds(k0, kb)], thr, need, run, tri_ref[...])
        s = _dot(qbd, kbuf[slot, :, pl.ds(k0, kb)].astype(BF16)) + tile_bias(bias, kb)
        vt = vbuf[slot, :, pl.ds(k0, kb)].astype(BF16)
        _softmax_tile(s, lambda p: _dot_nt(p, vt), m_ref, l_ref, acc_ref, 0)
        return run

    run = lax.fori_loop(0, n_tiles, attn_tile, jnp.zeros((t, 1), F32))
    bias, _ = _selection_bias(keys_ref[:, past:past + LANES], thr, need, run, tri_ref[:LANES, :LANES])
    s = _dot_nt(qbd, kan_ref[0]) + tile_bias(bias, LANES)
    van = van_ref[0]
    _softmax_tile(s, lambda p: _dot(p, van), m_ref, l_ref, acc_ref, 0)

    o = acc_ref[0] * (1.0 / _lane_tile(l_ref[0], acc_ref.shape[2]))
    for n in range(kv_heads):
        for g in range(group):
            h = n * group + g
            r0 = (n * group + g) * t
            o_ref[0, :, h * head_dim:(h + 1) * head_dim] = o[r0:r0 + t, n * head_dim:(n + 1) * head_dim]


def _sparse_attention_sample(page_table, qi, wi, qbd, ki_new, ka_new, va_new, tri, cache_idx_t, cache_k_t, cache_v_t,
                             topk, kv_heads, group, head_dim):
    db, t, _ = wi.shape
    n_pages = page_table.shape[1]
    page = cache_idx_t.shape[2]
    past = n_pages * page
    kb = min(SAMPLE_KB, past)
    rows = kv_heads * group * t
    kvw = kv_heads * head_dim
    cw = _count_chunk(t)
    kern = functools.partial(_sparse_sample_kernel, t=t, kb=kb, topk=topk, n_pages=n_pages, page=page,
                             kv_heads=kv_heads, group=group, head_dim=head_dim)
    per_b = lambda shape: pl.BlockSpec((1,) + shape, lambda b, pt: (b, 0, 0))
    grid_spec = pltpu.PrefetchScalarGridSpec(
        num_scalar_prefetch=1,
        grid=(db,),
        in_specs=[per_b(qi.shape[1:]), per_b(wi.shape[1:]), per_b(qbd.shape[1:]), per_b(ki_new.shape[1:]),
                  per_b(ka_new.shape[1:]), per_b(va_new.shape[1:]),
                  pl.BlockSpec(tri.shape, lambda b, pt: (0, 0)),
                  pl.BlockSpec(memory_space=pl.ANY), pl.BlockSpec(memory_space=pl.ANY),
                  pl.BlockSpec(memory_space=pl.ANY)],
        out_specs=per_b((t, kv_heads * group * head_dim)),
        scratch_shapes=[pltpu.VMEM((2, cache_idx_t.shape[1], past), F32),
                        pltpu.VMEM((2, kvw, past), F32),
                        pltpu.VMEM((2, kvw, past), F32),
                        pltpu.SemaphoreType.DMA((2, 3)),
                        pltpu.VMEM((t, past + cw), I32),
                        pltpu.VMEM((1, rows, LANES), F32),
                        pltpu.VMEM((1, rows, LANES), F32),
                        pltpu.VMEM((1, rows, kvw), F32)])
    return pl.pallas_call(
        kern,
        grid_spec=grid_spec,
        out_shape=jax.ShapeDtypeStruct((db, t, kv_heads * group * head_dim), F32),
        compiler_params=_params(),
        name="sparse_attn_sample",
    )(page_table, qi, wi, qbd, ki_new, ka_new, va_new, tri, cache_idx_t, cache_k_t, cache_v_t)


def _log_gamma(heads):
    return np.log(1.0 - np.exp2(-RET_DECAY_OFFSET - np.arange(heads, dtype=np.float64)))


def _retention_tables(heads, chunk, dk, group):
    lg = _log_gamma(heads)
    i = np.arange(chunk, dtype=np.float64)
    diff = i[:, None] - i[None, :]
    decay = np.where(diff >= 0, np.exp(lg[:, None, None] * np.maximum(diff, 0.0)), 0.0)
    decay_bd = np.zeros((heads, group * chunk, group * chunk))
    for gidx in range(group):
        decay_bd[:, gidx * chunk:(gidx + 1) * chunk, gidx * chunk:(gidx + 1) * chunk] = decay
    q_scale = np.tile(np.exp(lg[:, None] * (i + 1.0)), (1, group))
    k_scale = np.tile(np.exp(lg[:, None] * (chunk - 1.0 - i)), (1, group))
    state_scale = np.exp(lg * chunk)
    rows_q = np.broadcast_to(q_scale[:, :, None], (heads, group * chunk, dk))
    rows_k = np.broadcast_to(k_scale[:, :, None], (heads, group * chunk, dk))
    cols_k = np.broadcast_to(k_scale[:, None, :], (heads, dk, group * chunk))
    as32 = lambda a: jnp.asarray(np.ascontiguousarray(a), F32)
    return as32(decay_bd), as32(rows_q), as32(rows_k), as32(cols_k), [float(v) for v in state_scale]


def _groupnorm_gate(o, gn_g, g_r):
    mu = jnp.mean(o, axis=-1, keepdims=True)
    d = o - mu
    var = jnp.mean(d * d, axis=-1, keepdims=True)
    return (g_r * _sigmoid(g_r)) * ((d * lax.rsqrt(var + EPS)) * gn_g)


def _retention_prompt_kernel(q_ref, k_ref, v_ref, g_ref, gn_ref, dec_ref, qs_ref, ks_ref, o_ref, st_ref, state,
                             *, heads, dk, dv, state_scale):
    c = pl.program_id(1)

    @pl.when(c == 0)
    def _():
        state[...] = jnp.zeros(state.shape, F32)

    for h in range(heads):
        q = q_ref[0, :, h * dk:(h + 1) * dk]
        k = k_ref[0, :, h * dk:(h + 1) * dk]
        v = v_ref[0, :, h * dv:(h + 1) * dv].astype(BF16)
        s_old = state[h]
        inner = _dot_nt(q.astype(BF16), k.astype(BF16)) * dec_ref[h]
        o = _dot(inner.astype(BF16), v) + _dot((q * qs_ref[h]).astype(BF16), s_old.astype(BF16))
        state[h] = state_scale[h] * s_old + _dot((k * ks_ref[h]).T.astype(BF16), v)
        o_ref[0, :, h * dv:(h + 1) * dv] = _groupnorm_gate(o, gn_ref[:, h * dv:(h + 1) * dv],
                                                           g_ref[0, :, h * dv:(h + 1) * dv])

    @pl.when(c == pl.num_programs(1) - 1)
    def _():
        st_ref[0] = state[...]


def _retention_prompt(q, k, v, g, gn_g, heads, dk, dv):
    b, s, _ = q.shape
    chunk = min(RET_CHUNK, s)
    dec, qs, ks, _, state_scale = _retention_tables(heads, chunk, dk, 1)
    blk = lambda w: pl.BlockSpec((1, chunk, w), lambda bi, ci: (bi, ci, 0))
    const = lambda a: pl.BlockSpec(a.shape, lambda bi, ci: (0,) * a.ndim)
    gn2 = gn_g.reshape(1, heads * dv)
    kern = functools.partial(_retention_prompt_kernel, heads=heads, dk=dk, dv=dv, state_scale=state_scale)
    return pl.pallas_call(
        kern,
        grid=(b, s // chunk),
        in_specs=[blk(heads * dk), blk(heads * dk), blk(heads * dv), blk(heads * dv),
                  const(gn2), const(dec), const(qs), const(ks)],
        out_specs=[blk(heads * dv), pl.BlockSpec((1, heads, dk, dv), lambda bi, ci: (bi, 0, 0, 0))],
        out_shape=[jax.ShapeDtypeStruct((b, s, heads * dv), F32),
                   jax.ShapeDtypeStruct((b, heads, dk, dv), F32)],
        scratch_shapes=[pltpu.VMEM((heads, dk, dv), F32)],
        compiler_params=_params(2),
        name="retention_prompt",
    )(q, k, v, g, gn2, dec, qs, ks)


def _retention_sample_kernel(q_ref, k_ref, kt_ref, v_ref, g_ref, gn_ref, dec_ref, qs_ref, kst_ref, st_ref,
                             o_ref, sto_ref, *, heads, dk, dv, t, group, state_scale):
    rows = group * t
    col_batch = lax.broadcasted_iota(I32, (dk, rows), 1) // t
    for h in range(heads):
        q = q_ref[:, h * dk:(h + 1) * dk]
        k = k_ref[:, h * dk:(h + 1) * dk]
        v = v_ref[:, h * dv:(h + 1) * dv].astype(BF16)
        inner = _dot_nt(q.astype(BF16), k.astype(BF16)) * dec_ref[h]
        o = _dot(inner.astype(BF16), v)
        qd = (q * qs_ref[h]).astype(BF16)
        kdt = kt_ref[0, h] * kst_ref[h]
        cross = []
        for bi in range(group):
            s_old = st_ref[bi, h]
            cross.append(_dot(qd, s_old.astype(BF16))[bi * t:(bi + 1) * t])
            kd_b = jnp.where(col_batch == bi, kdt, 0.0).astype(BF16)
            sto_ref[bi, h] = state_scale[h] * s_old + _dot(kd_b, v)
        o = o + jnp.concatenate(cross, axis=0)
        o_ref[:, h * dv:(h + 1) * dv] = _groupnorm_gate(o, gn_ref[:, h * dv:(h + 1) * dv],
                                                        g_ref[:, h * dv:(h + 1) * dv])


def _retention_sample(state, q, k, v, g, gn_g, heads, dk, dv, t):
    db = state.shape[0]
    group = min(SAMPLE_GROUP, db)
    rows = group * t
    dec, qs, _, kst, state_scale = _retention_tables(heads, t, dk, group)
    kt = k.reshape(db // group, rows, heads, dk).transpose(0, 2, 3, 1)
    gn2 = gn_g.reshape(1, heads * dv)
    blk = lambda w: pl.BlockSpec((rows, w), lambda i: (i, 0))
    const = lambda a: pl.BlockSpec(a.shape, lambda i: (0,) * a.ndim)
    st_spec = pl.BlockSpec((group, heads, dk, dv), lambda i: (i, 0, 0, 0))
    kern = functools.partial(_retention_sample_kernel, heads=heads, dk=dk, dv=dv, t=t, group=group,
                             state_scale=state_scale)
    return pl.pallas_call(
        kern,
        grid=(db // group,),
        in_specs=[blk(heads * dk), blk(heads * dk),
                  pl.BlockSpec((1, heads, dk, rows), lambda i: (i, 0, 0, 0)),
                  blk(heads * dv), blk(heads * dv), const(gn2), const(dec), const(qs), const(kst), st_spec],
        out_specs=[blk(heads * dv), st_spec],
        out_shape=[jax.ShapeDtypeStruct((db * t, heads * dv), F32),
                   jax.ShapeDtypeStruct(state.shape, F32)],
        compiler_params=_params(),
        name="retention_sample",
    )(q, k, kt, v, g, gn2, dec, qs, kst, state)


def _mem_kv_kernel(x_ref, g_ref, w_ref, k_ref, v_ref):
    hb = _rmsnorm(x_ref[...], g_ref[...]).astype(BF16)
    half = w_ref.shape[1] // 2
    k_ref[...] = _dot(hb, w_ref[:, :half])
    v_ref[...] = _dot(hb, w_ref[:, half:])


def _memory_kv(mem2d, g, w_bf16):
    t, d = mem2d.shape
    half = w_bf16.shape[1] // 2
    tm = min(PROJ_ROWS, t)
    return pl.pallas_call(
        _mem_kv_kernel,
        grid=(t // tm,),
        in_specs=[pl.BlockSpec((tm, d), lambda i: (i, 0)), pl.BlockSpec((1, d), lambda i: (0, 0)), _whole_vmem()],
        out_specs=[pl.BlockSpec((tm, half), lambda i: (i, 0))] * 2,
        out_shape=[jax.ShapeDtypeStruct((t, half), F32)] * 2,
        compiler_params=_params(),
        name="memory_kv",
    )(mem2d, g.reshape(1, d), w_bf16)


def _mem_head_attention(q, k, v):
    s = _dot_nt(q, k)
    p = jnp.exp(s - jnp.max(s, axis=1, keepdims=True))
    return _dot(p.astype(BF16), v) * (1.0 / jnp.sum(p, axis=1, keepdims=True))


def _mem_attn_prompt_kernel(q_ref, k_ref, v_ref, o_ref, *, heads, dim):
    for h in range(heads):
        sl = slice(h * dim, (h + 1) * dim)
        o_ref[0, :, sl] = _mem_head_attention(q_ref[0, :, sl], k_ref[0, :, sl], v_ref[0, :, sl])


def _memory_attention_prompt(q, mk, mv, heads, dim):
    b, s, w = q.shape
    tm = min(MEM_ROWS, s)
    m = mk.shape[1]
    return pl.pallas_call(
        functools.partial(_mem_attn_prompt_kernel, heads=heads, dim=dim),
        grid=(b, s // tm),
        in_specs=[pl.BlockSpec((1, tm, w), lambda bi, i: (bi, i, 0)),
                  pl.BlockSpec((1, m, w), lambda bi, i: (bi, 0, 0)),
                  pl.BlockSpec((1, m, w), lambda bi, i: (bi, 0, 0))],
        out_specs=pl.BlockSpec((1, tm, w), lambda bi, i: (bi, i, 0)),
        out_shape=jax.ShapeDtypeStruct((b, s, w), F32),
        compiler_params=_params(2),
        name="memory_attn_prompt",
    )(q, mk, mv)


def _mem_attn_sample_kernel(q_ref, k_ref, v_ref, o_ref, *, heads, dim, t, group):
    for bi in range(group):
        for h in range(heads):
            sl = slice(h * dim, (h + 1) * dim)
            o = _mem_head_attention(q_ref[:, sl].astype(BF16), k_ref[bi, :, h, :].astype(BF16),
                                    v_ref[bi, :, h, :].astype(BF16))
            o_ref[bi * t:(bi + 1) * t, sl] = o[bi * t:(bi + 1) * t]


def _memory_attention_sample(q, mk, mv, t):
    db, m, heads, dim = mk.shape
    group = min(SAMPLE_GROUP, db)
    w = heads * dim
    mem_spec = pl.BlockSpec((group, m, heads, dim), lambda i: (i, 0, 0, 0))
    return pl.pallas_call(
        functools.partial(_mem_attn_sample_kernel, heads=heads, dim=dim, t=t, group=group),
        grid=(db // group,),
        in_specs=[pl.BlockSpec((group * t, w), lambda i: (i, 0)), mem_spec, mem_spec],
        out_specs=pl.BlockSpec((group * t, w), lambda i: (i, 0)),
        out_shape=jax.ShapeDtypeStruct((db * t, w), F32),
        compiler_params=_params(),
        name="memory_attn_sample",
    )(q, mk, mv)


def _merge_ffn_kernel(x_ref, gt_ref, oa_ref, or_ref, om_ref, wo_ref, fg_ref, wg_ref, wu_ref, wd_ref, fin_ref,
                      y_ref, *, final_norm):
    d = x_ref.shape[1]
    u = (gt_ref[:, :d] * oa_ref[...] + gt_ref[:, d:2 * d] * or_ref[...] + gt_ref[:, 2 * d:] * om_ref[...])
    x1 = x_ref[...] + _dot(u.astype(BF16), wo_ref[...])
    hb = _rmsnorm(x1, fg_ref[...]).astype(BF16)
    a = _dot(hb, wg_ref[...])
    act = (a * _sigmoid(a)) * _dot(hb, wu_ref[...])
    x2 = x1 + _dot(act.astype(BF16), wd_ref[...])
    y_ref[...] = _rmsnorm(x2, fin_ref[...]) if final_norm else x2


def _merge_ffn(x2d, gates, o_a, o_r, o_m, w_out, ffn_g, w_gate, w_up, w_down, final_g, final_norm):
    t, d = x2d.shape
    tm = min(PROJ_ROWS, t)
    row = lambda w: pl.BlockSpec((tm, w), lambda i: (i, 0))
    vec = pl.BlockSpec((1, d), lambda i: (0, 0))
    return pl.pallas_call(
        functools.partial(_merge_ffn_kernel, final_norm=final_norm),
        grid=(t // tm,),
        in_specs=[row(d), row(gates.shape[1]), row(d), row(d), row(d), _whole_vmem(), vec,
                  _whole_vmem(), _whole_vmem(), _whole_vmem(), vec],
        out_specs=row(d),
        out_shape=jax.ShapeDtypeStruct((t, d), F32),
        compiler_params=_params(),
        name="merge_ffn",
    )(x2d, gates, o_a, o_r, o_m, w_out, ffn_g.reshape(1, d), w_gate, w_up, w_down, final_g.reshape(1, d))


def kernel(x_prompt, x_sample, cache_k, cache_v, cache_idx_k, state_ret, cache_mem_k, cache_mem_v, page_table,
           mem_prompt, attn_norm_g, w_in, ret_gn_g, w_out, ffn_norm_g, w_gate, w_up, w_down, mem_norm_g, w_mem_kv,
           final_norm_g):
    depth = w_in.shape[0]
    b, s, d = x_prompt.shape
    db, t, _ = x_sample.shape
    _, n_pool, page, kv_heads, head_dim = cache_k.shape
    idx_dim = cache_idx_k.shape[-1]
    _, _, ret_heads, ret_dk, ret_dv = state_ret.shape
    _, _, mem_tokens, mem_heads, mem_dim = cache_mem_k.shape
    q_heads = d // head_dim
    group = q_heads // kv_heads
    kvw = kv_heads * head_dim
    n_pages = page_table.shape[1]
    past = n_pages * page
    dims = dict(idx_dim=idx_dim, ret_dk=ret_dk, head_dim=head_dim, mem_dim=mem_dim)
    lay, _ = _proj_layout(d, kvw, idx_dim, ret_heads * ret_dk, ret_heads * ret_dv, mem_heads * mem_dim,
                          w_in.shape[2] - (d + 2 * kvw + IDX_HEADS * idx_dim + idx_dim + IDX_HEADS
                                           + 2 * ret_heads * ret_dk + 2 * ret_heads * ret_dv + mem_heads * mem_dim))
    o_ki = d + 2 * kvw + IDX_HEADS * idx_dim
    o_qr = o_ki + idx_dim + IDX_HEADS

    cos_p, sin_p = _rotary_tables(jnp.arange(s), ret_dk, min(PROJ_ROWS, b * s))
    cos_s, sin_s = _rotary_tables(past + jnp.arange(t), ret_dk, min(PROJ_ROWS, db * t))
    tri = (jnp.arange(TRI_W)[:, None] <= jnp.arange(TRI_W)[None, :]).astype(BF16)
    topk_p = min(TOPK_MAX, s // 4)
    topk_s = min(TOPK_MAX, (past + t) // 4)

    xp = x_prompt.reshape(b * s, d)
    xs = x_sample.reshape(db * t, d)
    outs = {k: [] for k in ("k_p", "v_p", "ik_p", "rs_p", "mk_p", "mv_p", "k_s", "v_s", "ik_s", "rs_s")}
    for l in range(depth):
        last = l == depth - 1
        wl = w_in[l]
        w_packed = jnp.concatenate(
            [wl[:, :o_ki], wl[:, o_ki:o_qr], jnp.zeros((d, LANES - (o_qr - o_ki)), wl.dtype), wl[:, o_qr:]],
            axis=1).astype(BF16)
        w_out_b, w_gate_b, w_up_b, w_down_b = (a[l].astype(BF16) for a in (w_out, w_gate, w_up, w_down))

        p = _project(xp, attn_norm_g[l], w_packed, cos_p, sin_p, lay, dims, BF16)
        qi = p["qi"].reshape(b * s, IDX_HEADS, idx_dim).transpose(1, 0, 2)
        qa = p["qa"].reshape(b * s, q_heads, head_dim).transpose(1, 0, 2)
        kit = p["ki"].astype(BF16).reshape(b, s, idx_dim).transpose(0, 2, 1)
        kt = p["ka"].astype(BF16).reshape(b, s, kvw).transpose(0, 2, 1)
        va = p["va"].astype(BF16).reshape(b, s, kv_heads, head_dim)
        vb = jnp.concatenate([va, jnp.ones((b, s, kv_heads, LANES - head_dim), BF16)], axis=-1)
        o_a = _sparse_attention_prompt(qi, p["wi"], qa, kit, kt, vb.reshape(b, s, kv_heads * LANES), tri, topk_p,
                                       kv_heads, head_dim)
        o_r, st_p = _retention_prompt(p["qr"].reshape(b, s, -1), p["kr"].reshape(b, s, -1),
                                      p["vr"].reshape(b, s, -1), p["gr"].reshape(b, s, -1), ret_gn_g[l],
                                      ret_heads, ret_dk, ret_dv)
        mk, mv = _memory_kv(mem_prompt.reshape(b * mem_tokens, d), mem_norm_g[l], w_mem_kv[l].astype(BF16))
        o_m = _memory_attention_prompt(p["qm"].reshape(b, s, -1), mk.astype(BF16).reshape(b, mem_tokens, -1),
                                       mv.astype(BF16).reshape(b, mem_tokens, -1), mem_heads, mem_dim)
        xp = _merge_ffn(xp, p["gt"], o_a, o_r.reshape(b * s, -1), o_m.reshape(b * s, -1), w_out_b, ffn_norm_g[l],
                        w_gate_b, w_up_b, w_down_b, final_norm_g, last)
        outs["k_p"].append(p["ka"].reshape(b, s, kv_heads, head_dim))
        outs["v_p"].append(p["va"].reshape(b, s, kv_heads, head_dim))
        outs["ik_p"].append(p["ki"].reshape(b, s, idx_dim))
        outs["rs_p"].append(st_p)
        outs["mk_p"].append(mk.reshape(b, mem_tokens, mem_heads, mem_dim))
        outs["mv_p"].append(mv.reshape(b, mem_tokens, mem_heads, mem_dim))

        q = _project(xs, attn_norm_g[l], w_packed, cos_s, sin_s, lay, dims, F32)
        qi_s = q["qi"].reshape(db, t, IDX_HEADS, idx_dim).transpose(0, 2, 1, 3).reshape(db, IDX_HEADS * t, idx_dim)
        qa_s = q["qa"].reshape(db, t, kv_heads, group, head_dim).transpose(0, 2, 3, 1, 4)
        qbd = (qa_s.reshape(db, kv_heads, group * t, 1, head_dim)
               * jnp.eye(kv_heads, dtype=F32)[None, :, None, :, None]).reshape(db, kv_heads * group * t, kvw)
        pad_rows = lambda a: jnp.pad(a.reshape(db, t, -1), ((0, 0), (0, LANES - t), (0, 0))).astype(BF16)
        o_a = _sparse_attention_sample(
            page_table, qi_s.astype(BF16), q["wi"].reshape(db, t, IDX_HEADS), qbd.astype(BF16),
            pad_rows(q["ki"]), pad_rows(q["ka"]), pad_rows(q["va"]), tri,
            jnp.swapaxes(cache_idx_k[l], 1, 2),
            cache_k[l].transpose(0, 2, 3, 1).reshape(n_pool, kvw, page),
            cache_v[l].transpose(0, 2, 3, 1).reshape(n_pool, kvw, page),
            topk_s, kv_heads, group, head_dim).reshape(db * t, d)
        o_r, st_s = _retention_sample(state_ret[l], q["qr"], q["kr"], q["vr"], q["gr"], ret_gn_g[l],
                                      ret_heads, ret_dk, ret_dv, t)
        o_m = _memory_attention_sample(q["qm"], cache_mem_k[l], cache_mem_v[l], t)
        xs = _merge_ffn(xs, q["gt"], o_a, o_r, o_m, w_out_b, ffn_norm_g[l], w_gate_b, w_up_b, w_down_b,
                        final_norm_g, last)
        outs["k_s"].append(q["ka"].reshape(db, t, kv_heads, head_dim))
        outs["v_s"].append(q["va"].reshape(db, t, kv_heads, head_dim))
        outs["ik_s"].append(q["ki"].reshape(db, t, idx_dim))
        outs["rs_s"].append(st_s)

    stk = lambda k: jnp.stack(outs[k])
    return (xp.reshape(b, s, d), xs.reshape(db, t, d), stk("k_p"), stk("v_p"), stk("ik_p"), stk("rs_p"),
            stk("mk_p"), stk("mv_p"), stk("k_s"), stk("v_s"), stk("ik_s"), stk("rs_s"))
```

```python
import functools
import math

import numpy as np
import jax
import jax.numpy as jnp
from jax import lax
from jax.experimental import pallas as pl
from jax.experimental.pallas import tpu as pltpu

F32 = jnp.float32
BF16 = jnp.bfloat16
I32 = jnp.int32

IDX_HEADS = 4
TOPK_MAX = 256
RET_CHUNK = 128
RET_DECAY_OFFSET = 5.0
ROPE_BASE = 10000.0
EPS = 1e-6

LANES = 128
VMEM_LIMIT_BYTES = 56 * 1024 * 1024

NEG = -1e30
INT_MIN = -(2 ** 31)
LOG2E = math.log2(math.e)
MIN_NORMAL = 2.0 ** -126

PROJ_ROWS = 256
ATTN_TQ = 128
ATTN_KB = 1024
SAMPLE_KB = 1024
COUNT_ELEMS = 65536
TRI_W = 256
PREFILTER_DEPTH = 12
PREFILTER_MIN_CHUNKS = 8
MEM_ROWS = 512
SAMPLE_GROUP = 8


def _params(n_axes=1):
    return pltpu.CompilerParams(dimension_semantics=("arbitrary",) * n_axes,
                                vmem_limit_bytes=VMEM_LIMIT_BYTES)


def _whole_vmem():
    return pl.BlockSpec(memory_space=pltpu.VMEM)


def _rmsnorm(x, g):
    return (x * lax.rsqrt(jnp.mean(x * x, axis=-1, keepdims=True) + EPS)) * g


def _sigmoid(x):
    return 1.0 / (1.0 + jnp.exp(-x))


def _dot(a, b):
    return jnp.dot(a, b, preferred_element_type=F32)


def _dot_nt(a, b):
    return lax.dot_general(a, b, (((1,), (1,)), ((), ())), preferred_element_type=F32)


def _proj_layout(d_model, kv_w, idx_dim, ret_qk_w, ret_v_w, mem_w, n_gate):
    segs = [("qa", d_model), ("ka", kv_w), ("va", kv_w), ("qi", IDX_HEADS * idx_dim), ("kw", LANES),
            ("qr", ret_qk_w), ("kr", ret_qk_w), ("vr", ret_v_w), ("gr", ret_v_w), ("qm", mem_w),
            ("gt", n_gate)]
    out, off = {}, 0
    for name, w in segs:
        out[name] = (off, off + w)
        off += w
    return out, off


def _proj_kernel(x_ref, g_ref, w_ref, cos_ref, sin_ref,
                 qa_ref, ka_ref, va_ref, qi_ref, ki_ref, wi_ref, qr_ref, kr_ref, vr_ref, gr_ref, qm_ref, gt_ref,
                 *, lay, idx_dim, ret_dk, qa_scale, wi_scale, kr_scale, qm_scale):
    hb = _rmsnorm(x_ref[...], g_ref[...]).astype(BF16)

    def seg(name):
        lo, hi = lay[name]
        return _dot(hb, w_ref[:, lo:hi])

    qa_ref[...] = (seg("qa") * qa_scale).astype(qa_ref.dtype)
    ka_ref[...] = seg("ka")
    va_ref[...] = seg("va")
    qi_ref[...] = seg("qi").astype(qi_ref.dtype)
    kw = seg("kw")
    ki_ref[...] = kw[:, :idx_dim]
    wi_ref[...] = kw[:, idx_dim:idx_dim + IDX_HEADS] * wi_scale

    cos2 = cos_ref[...]
    sin2 = sin_ref[...]

    def rotary(z, out_ref, scale):
        for h in range(z.shape[1] // ret_dk):
            zh = z[:, h * ret_dk:(h + 1) * ret_dk]
            r = zh * cos2 + pltpu.roll(zh, ret_dk // 2, 1) * sin2
            out_ref[:, h * ret_dk:(h + 1) * ret_dk] = r if scale is None else r * scale

    rotary(seg("qr"), qr_ref, None)
    rotary(seg("kr"), kr_ref, kr_scale)
    vr_ref[...] = seg("vr")
    gr_ref[...] = seg("gr")
    qm_ref[...] = (seg("qm") * qm_scale).astype(qm_ref.dtype)
    gt_ref[...] = _sigmoid(seg("gt"))


def _project(x2d, g, w_packed, cos2, sin2, lay, dims, q_dtype):
    t, d = x2d.shape
    tm = min(PROJ_ROWS, t)
    n_pos_blocks = cos2.shape[0] // tm
    widths = {k: hi - lo for k, (lo, hi) in lay.items()}
    out_defs = [("qa", widths["qa"], q_dtype), ("ka", widths["ka"], F32), ("va", widths["va"], F32),
                ("qi", widths["qi"], q_dtype), ("ki", dims["idx_dim"], F32), ("wi", IDX_HEADS, F32),
                ("qr", widths["qr"], F32), ("kr", widths["kr"], F32), ("vr", widths["vr"], F32),
                ("gr", widths["gr"], F32), ("qm", widths["qm"], q_dtype), ("gt", widths["gt"], F32)]
    row = lambda w: pl.BlockSpec((tm, w), lambda i: (i, 0))
    kern = functools.partial(
        _proj_kernel, lay=lay, idx_dim=dims["idx_dim"], ret_dk=dims["ret_dk"],
        qa_scale=dims["head_dim"] ** -0.5 * LOG2E, wi_scale=(IDX_HEADS ** -0.5) * (dims["idx_dim"] ** -0.5),
        kr_scale=dims["ret_dk"] ** -0.5, qm_scale=dims["mem_dim"] ** -0.5)
    outs = pl.pallas_call(
        kern,
        grid=(t // tm,),
        in_specs=[row(d), pl.BlockSpec((1, d), lambda i: (0, 0)), _whole_vmem(),
                  pl.BlockSpec((tm, LANES), lambda i: (i % n_pos_blocks, 0)),
                  pl.BlockSpec((tm, LANES), lambda i: (i % n_pos_blocks, 0))],
        out_specs=[row(w) for _, w, _ in out_defs],
        out_shape=[jax.ShapeDtypeStruct((t, w), dt) for _, w, dt in out_defs],
        compiler_params=_params(),
        name="proj",
    )(x2d, g.reshape(1, d), w_packed, cos2, sin2)
    return {name: o for (name, _, _), o in zip(out_defs, outs)}


def _rotary_tables(pos, ret_dk, rows):
    half = ret_dk // 2
    inv = ROPE_BASE ** (-jnp.arange(half, dtype=F32) / half)
    ang = pos.astype(F32)[:, None] * inv[None, :]
    cos, sin = jnp.cos(ang), jnp.sin(ang)
    cos2 = jnp.concatenate([cos, cos], axis=1)
    sin2 = jnp.concatenate([-sin, sin], axis=1)
    reps = max(1, rows // cos2.shape[0])
    return jnp.tile(cos2, (reps, 1)), jnp.tile(sin2, (reps, 1))


def _index_scores(s4, w):
    acc = w[:, 0:1] * jnp.maximum(s4[0], 0.0)
    for h in range(1, IDX_HEADS):
        acc = acc + w[:, h:h + 1] * jnp.maximum(s4[h], 0.0)
    return jnp.where(jnp.abs(acc) < MIN_NORMAL, 0.0, acc)


def _sortable_key(bits):
    return jnp.where(bits >= 0, bits, bits ^ 0x7FFFFFFF)


def _count_chunk(rows):
    return max(LANES, COUNT_ELEMS // rows)


def _tree_sum(parts):
    while len(parts) > 1:
        parts = [a + b for a, b in zip(parts[::2], parts[1::2])] + parts[len(parts) & ~1:]
    return parts[0]


def _count_ge(keys_ref, rows, n_chunks, cand):
    cw = _count_chunk(rows)
    cb = jnp.broadcast_to(cand, (rows, LANES))

    def body(c, a):
        blk = keys_ref[:, pl.ds(pl.multiple_of(c * cw, cw), cw)]
        return a + _tree_sum([jnp.where(blk[:, u * LANES:(u + 1) * LANES] >= cb, 1.0, 0.0)
                              for u in range(cw // LANES)])

    a = lax.fori_loop(0, n_chunks, body, jnp.zeros((rows, LANES), F32))
    return jnp.sum(a, axis=1, keepdims=True)


def _kth_largest_key(keys_ref, rows, n_chunks, topk):
    def bit_step(b, u):
        cand_u = u | lax.shift_left(jnp.int32(1), 31 - b)
        return jnp.where(_count_ge(keys_ref, rows, n_chunks, cand_u ^ INT_MIN) >= topk, cand_u, u)

    return lax.fori_loop(0, 32, bit_step, jnp.zeros((rows, 1), I32)) ^ INT_MIN


def _tie_quota(thr, n_greater, topk):
    return jnp.where(thr == INT_MIN, 0.0, topk - n_greater)


def _topk_threshold(keys_ref, rows, n_chunks, topk):
    thr = _kth_largest_key(keys_ref, rows, n_chunks, topk)
    return thr, _tie_quota(thr, _count_ge(keys_ref, rows, n_chunks, thr + 1), topk)


def _lane_top_insert(cand_ref, sc_ref, rows, width):
    depth = cand_ref.shape[0]

    def body(g, c):
        r = pl.ds(pl.multiple_of(g * 8, 8), 8)
        tops = [cand_ref[m, r, :] for m in range(depth)]
        for c0 in range(0, width, LANES):
            x = sc_ref[r, c0:c0 + LANES]
            for m in range(depth):
                tops[m], x = jnp.maximum(tops[m], x), jnp.minimum(tops[m], x)
        for m in range(depth):
            cand_ref[m, r, :] = tops[m]
        return c

    lax.fori_loop(0, rows // 8, body, 0)


def _selection_bias(keys, thr, need, run, tri):
    tw = tri.shape[0]
    eq = keys == thr
    gt = keys > thr
    eq_b = jnp.where(eq, 1.0, 0.0).astype(BF16)
    parts = []
    for c in range(keys.shape[1] // tw):
        pre = _dot(eq_b[:, c * tw:(c + 1) * tw], tri) + run
        run = pre[:, tw - 1:]
        parts.append(pre)
    prefix = parts[0] if len(parts) == 1 else jnp.concatenate(parts, axis=1)
    bias = jnp.where(gt, 0.0, jnp.where(eq, jnp.where(prefix <= need, 0.0, NEG), NEG))
    return bias, run


def _lane_tile(x, width):
    reps = width // x.shape[1]
    return x if reps == 1 else jnp.concatenate([x] * reps, axis=1)


def _softmax_tile(s, pv, m_ref, l_ref, acc_ref, idx):
    m_prev = m_ref[idx]
    m_new = jnp.maximum(m_prev, jnp.max(s, axis=1, keepdims=True))
    alpha = jnp.exp2(m_prev - m_new)
    p = jnp.exp2(s - _lane_tile(m_new, s.shape[1]))
    if l_ref is not None:
        l_ref[idx] = alpha * l_ref[idx] + jnp.sum(p, axis=1, keepdims=True)
    acc = acc_ref[idx]
    acc_ref[idx] = _lane_tile(alpha, acc.shape[1]) * acc + pv(p.astype(BF16))
    m_ref[idx] = m_new


def _sparse_prompt_kernel(qi_ref, wi_ref, qa_ref, kit_ref, kt_ref, v_ref, tri_ref, o_ref,
                          keys_ref, sc_ref, cand_ref, ckeys_ref, m_ref, acc_ref,
                          *, tq, kb, topk, kv_heads, group, head_dim):
    i = pl.program_id(1)
    n_tiles = ((i + 1) * tq + kb - 1) // kb
    row_pos = i * tq + lax.broadcasted_iota(I32, (tq, kb), 0)
    col_iota = lax.broadcasted_iota(I32, (tq, kb), 1)
    w = wi_ref[...]
    qi = qi_ref[...].reshape(IDX_HEADS * tq, qi_ref.shape[2])
    cw = _count_chunk(tq)
    n_chunks = n_tiles * (kb // cw)
    depth = cand_ref.shape[0]
    cand_chunks = depth * LANES // cw
    prefilter = n_chunks >= PREFILTER_MIN_CHUNKS

    @pl.when(prefilter)
    def _():
        cand_ref[...] = jnp.full(cand_ref.shape, -jnp.inf, F32)

    def score_tile(j, c):
        k0 = pl.multiple_of(j * kb, kb)
        s4 = _dot(qi, kit_ref[0, :, pl.ds(k0, kb)]).reshape(IDX_HEADS, tq, kb)
        sc = _index_scores(s4, w)
        visible = k0 + col_iota <= row_pos
        keys_ref[:, pl.ds(k0, kb)] = jnp.where(visible, _sortable_key(lax.bitcast_convert_type(sc, I32)), INT_MIN)

        @pl.when(prefilter)
        def _():
            sc_ref[...] = jnp.where(visible, sc, -jnp.inf)
            _lane_top_insert(cand_ref, sc_ref, tq, kb)

        return c

    lax.fori_loop(0, n_tiles, score_tile, 0)

    def ranked_directly():
        return _topk_threshold(keys_ref, tq, n_chunks, topk)

    def ranked_from_lane_tops():
        for m in range(depth):
            ckeys_ref[:, m * LANES:(m + 1) * LANES] = _sortable_key(lax.bitcast_convert_type(cand_ref[m], I32))
        thr = _kth_largest_key(ckeys_ref, tq, cand_chunks, topk)
        n_greater = _count_ge(keys_ref, tq, n_chunks, thr + 1)
        kept = _count_ge(ckeys_ref, tq, cand_chunks, thr + 1)
        missed = jnp.max(n_greater - kept)
        return lax.cond(missed == 0.0, lambda: (thr, _tie_quota(thr, n_greater, topk)), ranked_directly)

    thr, need = lax.cond(prefilter, ranked_from_lane_tops, ranked_directly)

    m_ref[...] = jnp.full(m_ref.shape, NEG, F32)
    acc_ref[...] = jnp.zeros(acc_ref.shape, F32)

    def attn_tile(j, run):
        k0 = pl.multiple_of(j * kb, kb)
        bias, run = _selection_bias(keys_ref[:, pl.ds(k0, kb)], thr, need, run, tri_ref[...])
        for n in range(kv_heads):
            qn = qa_ref[n * group:(n + 1) * group].reshape(group * tq, head_dim)
            s = _dot(qn, kt_ref[0, n * head_dim:(n + 1) * head_dim, pl.ds(k0, kb)])
            s = (s.reshape(group, tq, kb) + bias[None]).reshape(group * tq, kb)
            vn = v_ref[0, pl.ds(k0, kb), n * LANES:(n + 1) * LANES]
            _softmax_tile(s, lambda p: _dot(p, vn), m_ref, None, acc_ref, n)
        return run

    lax.fori_loop(0, n_tiles, attn_tile, jnp.zeros((tq, 1), F32))

    for n in range(kv_heads):
        acc = acc_ref[n]
        o = acc * pltpu.roll(1.0 / acc, LANES - head_dim, 1)
        for g in range(group):
            h = n * group + g
            o_ref[:, h * head_dim:(h + 1) * head_dim] = o[g * tq:(g + 1) * tq, :head_dim]


def _sparse_attention_prompt(qi, wi, qa, kit, kt, v, tri, topk, kv_heads, head_dim):
    q_heads = qa.shape[0]
    b, s, _ = v.shape
    tq, kb = min(ATTN_TQ, s), min(ATTN_KB, s)
    nq = s // tq
    group = q_heads // kv_heads
    kern = functools.partial(_sparse_prompt_kernel, tq=tq, kb=kb, topk=topk, kv_heads=kv_heads,
                             group=group, head_dim=head_dim)
    resident = lambda shape: pl.BlockSpec((1,) + shape, lambda bi, i: (bi, 0, 0), pipeline_mode=pl.Buffered(1))
    return pl.pallas_call(
        kern,
        grid=(b, nq),
        in_specs=[pl.BlockSpec((IDX_HEADS, tq, qi.shape[2]), lambda bi, i: (0, bi * nq + i, 0)),
                  pl.BlockSpec((tq, IDX_HEADS), lambda bi, i: (bi * nq + i, 0)),
                  pl.BlockSpec((q_heads, tq, head_dim), lambda bi, i: (0, bi * nq + i, 0)),
                  resident(kit.shape[1:]), resident(kt.shape[1:]), resident(v.shape[1:]), _whole_vmem()],
        out_specs=pl.BlockSpec((tq, q_heads * head_dim), lambda bi, i: (bi * nq + i, 0)),
        out_shape=jax.ShapeDtypeStruct((b * s, q_heads * head_dim), F32),
        scratch_shapes=[pltpu.VMEM((tq, s), I32),
                        pltpu.VMEM((tq, kb), F32),
                        pltpu.VMEM((PREFILTER_DEPTH, tq, LANES), F32),
                        pltpu.VMEM((tq, PREFILTER_DEPTH * LANES), I32),
                        pltpu.VMEM((kv_heads, group * tq, LANES), F32),
                        pltpu.VMEM((kv_heads, group * tq, LANES), F32)],
        compiler_params=_params(2),
        name="sparse_attn_prompt",
    )(qi, wi, qa, kit, kt, v, tri)


def _sparse_sample_kernel(pt_ref, qi_ref, wi_ref, qbd_ref, kin_ref, kan_ref, van_ref, tri_ref,
                          cidx_hbm, ck_hbm, cv_hbm, o_ref,
                          ibuf, kbuf, vbuf, sems, keys_ref, m_ref, l_ref, acc_ref,
                          *, t, kb, topk, n_pages, page, kv_heads, group, head_dim):
    b = pl.program_id(0)
    nb = pl.num_programs(0)
    past = n_pages * page
    rows = kv_heads * group * t
    cw = _count_chunk(t)

    def page_copies(bb, slot, p):
        pg = pt_ref[bb, p]
        dst = pl.ds(pl.multiple_of(p * page, page), page)
        return (pltpu.make_async_copy(cidx_hbm.at[pg], ibuf.at[slot, :, dst], sems.at[slot, 0]),
                pltpu.make_async_copy(ck_hbm.at[pg], kbuf.at[slot, :, dst], sems.at[slot, 1]),
                pltpu.make_async_copy(cv_hbm.at[pg], vbuf.at[slot, :, dst], sems.at[slot, 2]))

    def start_fetch(bb, slot):
        def body(p, c):
            for cp in page_copies(bb, slot, p):
                cp.start()
            return c
        lax.fori_loop(0, n_pages, body, 0)

    def wait_fetch(bb, slot):
        def body(p, c):
            for cp in page_copies(bb, slot, p):
                cp.wait()
            return c
        lax.fori_loop(0, n_pages, body, 0)

    slot = b % 2

    @pl.when(b == 0)
    def _():
        start_fetch(b, slot)

    @pl.when(b + 1 < nb)
    def _():
        start_fetch(b + 1, 1 - slot)

    wait_fetch(b, slot)

    w = wi_ref[0]
    qi = qi_ref[0]
    n_tiles = past // kb

    def score_tile(j, c):
        k0 = pl.multiple_of(j * kb, kb)
        s4 = _dot(qi, ibuf[slot, :, pl.ds(k0, kb)].astype(BF16)).reshape(IDX_HEADS, t, kb)
        keys_ref[:, pl.ds(k0, kb)] = _sortable_key(lax.bitcast_convert_type(_index_scores(s4, w), I32))
        return c

    lax.fori_loop(0, n_tiles, score_tile, 0)
    s4 = _dot_nt(qi, kin_ref[0]).reshape(IDX_HEADS, t, LANES)
    key_new = _sortable_key(lax.bitcast_convert_type(_index_scores(s4, w), I32))
    new_col = lax.broadcasted_iota(I32, (t, LANES), 1)
    new_row = lax.broadcasted_iota(I32, (t, LANES), 0)
    keys_ref[:, past:past + LANES] = jnp.where(new_col <= new_row, key_new, INT_MIN)
    keys_ref[:, past + LANES:past + cw] = jnp.full((t, cw - LANES), INT_MIN, I32)

    thr, need = _topk_threshold(keys_ref, t, past // cw + 1, topk)

    m_ref[...] = jnp.full(m_ref.shape, NEG, F32)
    l_ref[...] = jnp.zeros(l_ref.shape, F32)
    acc_ref[...] = jnp.zeros(acc_ref.shape, F32)
    qbd = qbd_ref[0]

    def tile_bias(bias, width):
        return jnp.broadcast_to(bias[None], (rows // t, t, width)).reshape(rows, width)

    def attn_tile(j, run):
        k0 = pl.multiple_of(j * kb, kb)
        bias, run = _selection_bias(keys_ref[:, pl.ds(k0, kb)], thr, need, run, tri_ref[...])
        s = _dot(qbd, kbuf[slot, :, pl.ds(k0, kb)].astype(BF16)) + tile_bias(bias, kb)
        vt = vbuf[slot, :, pl.ds(k0, kb)].astype(BF16)
        _softmax_tile(s, lambda p: _dot_nt(p, vt), m_ref, l_ref, acc_ref, 0)
        return run

    run = lax.fori_loop(0, n_tiles, attn_tile, jnp.zeros((t, 1), F32))
    bias, _ = _selection_bias(keys_ref[:, past:past + LANES], thr, need, run, tri_ref[:LANES, :LANES])
    s = _dot_nt(qbd, kan_ref[0]) + tile_bias(bias, LANES)
    van = van_ref[0]
    _softmax_tile(s, lambda p: _dot(p, van), m_ref, l_ref, acc_ref, 0)

    o = acc_ref[0] * (1.0 / _lane_tile(l_ref[0], acc_ref.shape[2]))
    for n in range(kv_heads):
        for g in range(group):
            h = n * group + g
            r0 = (n * group + g) * t
            o_ref[0, :, h * head_dim:(h + 1) * head_dim] = o[r0:r0 + t, n * head_dim:(n + 1) * head_dim]


def _sparse_attention_sample(page_table, qi, wi, qbd, ki_new, ka_new, va_new, tri, cache_idx_t, cache_k_t, cache_v_t,
                             topk, kv_heads, group, head_dim):
    db, t, _ = wi.shape
    n_pages = page_table.shape[1]
    page = cache_idx_t.shape[2]
    past = n_pages * page
    kb = min(SAMPLE_KB, past)
    rows = kv_heads * group * t
    kvw = kv_heads * head_dim
    cw = _count_chunk(t)
    kern = functools.partial(_sparse_sample_kernel, t=t, kb=kb, topk=topk, n_pages=n_pages, page=page,
                             kv_heads=kv_heads, group=group, head_dim=head_dim)
    per_b = lambda shape: pl.BlockSpec((1,) + shape, lambda b, pt: (b, 0, 0))
    grid_spec = pltpu.PrefetchScalarGridSpec(
        num_scalar_prefetch=1,
        grid=(db,),
        in_specs=[per_b(qi.shape[1:]), per_b(wi.shape[1:]), per_b(qbd.shape[1:]), per_b(ki_new.shape[1:]),
                  per_b(ka_new.shape[1:]), per_b(va_new.shape[1:]),
                  pl.BlockSpec(tri.shape, lambda b, pt: (0, 0)),
                  pl.BlockSpec(memory_space=pl.ANY), pl.BlockSpec(memory_space=pl.ANY),
                  pl.BlockSpec(memory_space=pl.ANY)],
        out_specs=per_b((t, kv_heads * group * head_dim)),
        scratch_shapes=[pltpu.VMEM((2, cache_idx_t.shape[1], past), F32),
                        pltpu.VMEM((2, kvw, past), F32),
                        pltpu.VMEM((2, kvw, past), F32),
                        pltpu.SemaphoreType.DMA((2, 3)),
                        pltpu.VMEM((t, past + cw), I32),
                        pltpu.VMEM((1, rows, LANES), F32),
                        pltpu.VMEM((1, rows, LANES), F32),
                        pltpu.VMEM((1, rows, kvw), F32)])
    return pl.pallas_call(
        kern,
        grid_spec=grid_spec,
        out_shape=jax.ShapeDtypeStruct((db, t, kv_heads * group * head_dim), F32),
        compiler_params=_params(),
        name="sparse_attn_sample",
    )(page_table, qi, wi, qbd, ki_new, ka_new, va_new, tri, cache_idx_t, cache_k_t, cache_v_t)


def _log_gamma(heads):
    return np.log(1.0 - np.exp2(-RET_DECAY_OFFSET - np.arange(heads, dtype=np.float64)))


def _retention_tables(heads, chunk, dk, group):
    lg = _log_gamma(heads)
    i = np.arange(chunk, dtype=np.float64)
    diff = i[:, None] - i[None, :]
    decay = np.where(diff >= 0, np.exp(lg[:, None, None] * np.maximum(diff, 0.0)), 0.0)
    decay_bd = np.zeros((heads, group * chunk, group * chunk))
    for gidx in range(group):
        decay_bd[:, gidx * chunk:(gidx + 1) * chunk, gidx * chunk:(gidx + 1) * chunk] = decay
    q_scale = np.tile(np.exp(lg[:, None] * (i + 1.0)), (1, group))
    k_scale = np.tile(np.exp(lg[:, None] * (chunk - 1.0 - i)), (1, group))
    state_scale = np.exp(lg * chunk)
    rows_q = np.broadcast_to(q_scale[:, :, None], (heads, group * chunk, dk))
    rows_k = np.broadcast_to(k_scale[:, :, None], (heads, group * chunk, dk))
    cols_k = np.broadcast_to(k_scale[:, None, :], (heads, dk, group * chunk))
    as32 = lambda a: jnp.asarray(np.ascontiguousarray(a), F32)
    return as32(decay_bd), as32(rows_q), as32(rows_k), as32(cols_k), [float(v) for v in state_scale]


def _groupnorm_gate(o, gn_g, g_r):
    mu = jnp.mean(o, axis=-1, keepdims=True)
    d = o - mu
    var = jnp.mean(d * d, axis=-1, keepdims=True)
    return (g_r * _sigmoid(g_r)) * ((d * lax.rsqrt(var + EPS)) * gn_g)


def _retention_prompt_kernel(q_ref, k_ref, v_ref, g_ref, gn_ref, dec_ref, qs_ref, ks_ref, o_ref, st_ref, state,
                             *, heads, dk, dv, state_scale):
    c = pl.program_id(1)

    @pl.when(c == 0)
    def _():
        state[...] = jnp.zeros(state.shape, F32)

    for h in range(heads):
        q = q_ref[0, :, h * dk:(h + 1) * dk]
        k = k_ref[0, :, h * dk:(h + 1) * dk]
        v = v_ref[0, :, h * dv:(h + 1) * dv].astype(BF16)
        s_old = state[h]
        inner = _dot_nt(q.astype(BF16), k.astype(BF16)) * dec_ref[h]
        o = _dot(inner.astype(BF16), v) + _dot((q * qs_ref[h]).astype(BF16), s_old.astype(BF16))
        state[h] = state_scale[h] * s_old + _dot((k * ks_ref[h]).T.astype(BF16), v)
        o_ref[0, :, h * dv:(h + 1) * dv] = _groupnorm_gate(o, gn_ref[:, h * dv:(h + 1) * dv],
                                                           g_ref[0, :, h * dv:(h + 1) * dv])

    @pl.when(c == pl.num_programs(1) - 1)
    def _():
        st_ref[0] = state[...]


def _retention_prompt(q, k, v, g, gn_g, heads, dk, dv):
    b, s, _ = q.shape
    chunk = min(RET_CHUNK, s)
    dec, qs, ks, _, state_scale = _retention_tables(heads, chunk, dk, 1)
    blk = lambda w: pl.BlockSpec((1, chunk, w), lambda bi, ci: (bi, ci, 0))
    const = lambda a: pl.BlockSpec(a.shape, lambda bi, ci: (0,) * a.ndim)
    gn2 = gn_g.reshape(1, heads * dv)
    kern = functools.partial(_retention_prompt_kernel, heads=heads, dk=dk, dv=dv, state_scale=state_scale)
    return pl.pallas_call(
        kern,
        grid=(b, s // chunk),
        in_specs=[blk(heads * dk), blk(heads * dk), blk(heads * dv), blk(heads * dv),
                  const(gn2), const(dec), const(qs), const(ks)],
        out_specs=[blk(heads * dv), pl.BlockSpec((1, heads, dk, dv), lambda bi, ci: (bi, 0, 0, 0))],
        out_shape=[jax.ShapeDtypeStruct((b, s, heads * dv), F32),
                   jax.ShapeDtypeStruct((b, heads, dk, dv), F32)],
        scratch_shapes=[pltpu.VMEM((heads, dk, dv), F32)],
        compiler_params=_params(2),
        name="retention_prompt",
    )(q, k, v, g, gn2, dec, qs, ks)


def _retention_sample_kernel(q_ref, k_ref, kt_ref, v_ref, g_ref, gn_ref, dec_ref, qs_ref, kst_ref, st_ref,
                             o_ref, sto_ref, *, heads, dk, dv, t, group, state_scale):
    rows = group * t
    col_batch = lax.broadcasted_iota(I32, (dk, rows), 1) // t
    for h in range(heads):
        q = q_ref[:, h * dk:(h + 1) * dk]
        k = k_ref[:, h * dk:(h + 1) * dk]
        v = v_ref[:, h * dv:(h + 1) * dv].astype(BF16)
        inner = _dot_nt(q.astype(BF16), k.astype(BF16)) * dec_ref[h]
        o = _dot(inner.astype(BF16), v)
        qd = (q * qs_ref[h]).astype(BF16)
        kdt = kt_ref[0, h] * kst_ref[h]
        cross = []
        for bi in range(group):
            s_old = st_ref[bi, h]
            cross.append(_dot(qd, s_old.astype(BF16))[bi * t:(bi + 1) * t])
            kd_b = jnp.where(col_batch == bi, kdt, 0.0).astype(BF16)
            sto_ref[bi, h] = state_scale[h] * s_old + _dot(kd_b, v)
        o = o + jnp.concatenate(cross, axis=0)
        o_ref[:, h * dv:(h + 1) * dv] = _groupnorm_gate(o, gn_ref[:, h * dv:(h + 1) * dv],
                                                        g_ref[:, h * dv:(h + 1) * dv])


def _retention_sample(state, q, k, v, g, gn_g, heads, dk, dv, t):
    db = state.shape[0]
    group = min(SAMPLE_GROUP, db)
    rows = group * t
    dec, qs, _, kst, state_scale = _retention_tables(heads, t, dk, group)
    kt = k.reshape(db // group, rows, heads, dk).transpose(0, 2, 3, 1)
    gn2 = gn_g.reshape(1, heads * dv)
    blk = lambda w: pl.BlockSpec((rows, w), lambda i: (i, 0))
    const = lambda a: pl.BlockSpec(a.shape, lambda i: (0,) * a.ndim)
    st_spec = pl.BlockSpec((group, heads, dk, dv), lambda i: (i, 0, 0, 0))
    kern = functools.partial(_retention_sample_kernel, heads=heads, dk=dk, dv=dv, t=t, group=group,
                             state_scale=state_scale)
    return pl.pallas_call(
        kern,
        grid=(db // group,),
        in_specs=[blk(heads * dk), blk(heads * dk),
                  pl.BlockSpec((1, heads, dk, rows), lambda i: (i, 0, 0, 0)),
                  blk(heads * dv), blk(heads * dv), const(gn2), const(dec), const(qs), const(kst), st_spec],
        out_specs=[blk(heads * dv), st_spec],
        out_shape=[jax.ShapeDtypeStruct((db * t, heads * dv), F32),
                   jax.ShapeDtypeStruct(state.shape, F32)],
        compiler_params=_params(),
        name="retention_sample",
    )(q, k, kt, v, g, gn2, dec, qs, kst, state)


def _mem_kv_kernel(x_ref, g_ref, w_ref, k_ref, v_ref):
    hb = _rmsnorm(x_ref[...], g_ref[...]).astype(BF16)
    half = w_ref.shape[1] // 2
    k_ref[...] = _dot(hb, w_ref[:, :half])
    v_ref[...] = _dot(hb, w_ref[:, half:])


def _memory_kv(mem2d, g, w_bf16):
    t, d = mem2d.shape
    half = w_bf16.shape[1] // 2
    tm = min(PROJ_ROWS, t)
    return pl.pallas_call(
        _mem_kv_kernel,
        grid=(t // tm,),
        in_specs=[pl.BlockSpec((tm, d), lambda i: (i, 0)), pl.BlockSpec((1, d), lambda i: (0, 0)), _whole_vmem()],
        out_specs=[pl.BlockSpec((tm, half), lambda i: (i, 0))] * 2,
        out_shape=[jax.ShapeDtypeStruct((t, half), F32)] * 2,
        compiler_params=_params(),
        name="memory_kv",
    )(mem2d, g.reshape(1, d), w_bf16)


def _mem_head_attention(q, k, v):
    s = _dot_nt(q, k)
    p = jnp.exp(s - jnp.max(s, axis=1, keepdims=True))
    return _dot(p.astype(BF16), v) * (1.0 / jnp.sum(p, axis=1, keepdims=True))


def _mem_attn_prompt_kernel(q_ref, k_ref, v_ref, o_ref, *, heads, dim):
    for h in range(heads):
        sl = slice(h * dim, (h + 1) * dim)
        o_ref[0, :, sl] = _mem_head_attention(q_ref[0, :, sl], k_ref[0, :, sl], v_ref[0, :, sl])


def _memory_attention_prompt(q, mk, mv, heads, dim):
    b, s, w = q.shape
    tm = min(MEM_ROWS, s)
    m = mk.shape[1]
    return pl.pallas_call(
        functools.partial(_mem_attn_prompt_kernel, heads=heads, dim=dim),
        grid=(b, s // tm),
        in_specs=[pl.BlockSpec((1, tm, w), lambda bi, i: (bi, i, 0)),
                  pl.BlockSpec((1, m, w), lambda bi, i: (bi, 0, 0)),
                  pl.BlockSpec((1, m, w), lambda bi, i: (bi, 0, 0))],
        out_specs=pl.BlockSpec((1, tm, w), lambda bi, i: (bi, i, 0)),
        out_shape=jax.ShapeDtypeStruct((b, s, w), F32),
        compiler_params=_params(2),
        name="memory_attn_prompt",
    )(q, mk, mv)


def _mem_attn_sample_kernel(q_ref, k_ref, v_ref, o_ref, *, heads, dim, t, group):
    for bi in range(group):
        for h in range(heads):
            sl = slice(h * dim, (h + 1) * dim)
            o = _mem_head_attention(q_ref[:, sl].astype(BF16), k_ref[bi, :, h, :].astype(BF16),
                                    v_ref[bi, :, h, :].astype(BF16))
            o_ref[bi * t:(bi + 1) * t, sl] = o[bi * t:(bi + 1) * t]


def _memory_attention_sample(q, mk, mv, t):
    db, m, heads, dim = mk.shape
    group = min(SAMPLE_GROUP, db)
    w = heads * dim
    mem_spec = pl.BlockSpec((group, m, heads, dim), lambda i: (i, 0, 0, 0))
    return pl.pallas_call(
        functools.partial(_mem_attn_sample_kernel, heads=heads, dim=dim, t=t, group=group),
        grid=(db // group,),
        in_specs=[pl.BlockSpec((group * t, w), lambda i: (i, 0)), mem_spec, mem_spec],
        out_specs=pl.BlockSpec((group * t, w), lambda i: (i, 0)),
        out_shape=jax.ShapeDtypeStruct((db * t, w), F32),
        compiler_params=_params(),
        name="memory_attn_sample",
    )(q, mk, mv)


def _merge_ffn_kernel(x_ref, gt_ref, oa_ref, or_ref, om_ref, wo_ref, fg_ref, wg_ref, wu_ref, wd_ref, fin_ref,
                      y_ref, *, final_norm):
    d = x_ref.shape[1]
    u = (gt_ref[:, :d] * oa_ref[...] + gt_ref[:, d:2 * d] * or_ref[...] + gt_ref[:, 2 * d:] * om_ref[...])
    x1 = x_ref[...] + _dot(u.astype(BF16), wo_ref[...])
    hb = _rmsnorm(x1, fg_ref[...]).astype(BF16)
    a = _dot(hb, wg_ref[...])
    act = (a * _sigmoid(a)) * _dot(hb, wu_ref[...])
    x2 = x1 + _dot(act.astype(BF16), wd_ref[...])
    y_ref[...] = _rmsnorm(x2, fin_ref[...]) if final_norm else x2


def _merge_ffn(x2d, gates, o_a, o_r, o_m, w_out, ffn_g, w_gate, w_up, w_down, final_g, final_norm):
    t, d = x2d.shape
    tm = min(PROJ_ROWS, t)
    row = lambda w: pl.BlockSpec((tm, w), lambda i: (i, 0))
    vec = pl.BlockSpec((1, d), lambda i: (0, 0))
    return pl.pallas_call(
        functools.partial(_merge_ffn_kernel, final_norm=final_norm),
        grid=(t // tm,),
        in_specs=[row(d), row(gates.shape[1]), row(d), row(d), row(d), _whole_vmem(), vec,
                  _whole_vmem(), _whole_vmem(), _whole_vmem(), vec],
        out_specs=row(d),
        out_shape=jax.ShapeDtypeStruct((t, d), F32),
        compiler_params=_params(),
        name="merge_ffn",
    )(x2d, gates, o_a, o_r, o_m, w_out, ffn_g.reshape(1, d), w_gate, w_up, w_down, final_g.reshape(1, d))


def kernel(x_prompt, x_sample, cache_k, cache_v, cache_idx_k, state_ret, cache_mem_k, cache_mem_v, page_table,
           mem_prompt, attn_norm_g, w_in, ret_gn_g, w_out, ffn_norm_g, w_gate, w_up, w_down, mem_norm_g, w_mem_kv,
           final_norm_g):
    depth = w_in.shape[0]
    b, s, d = x_prompt.shape
    db, t, _ = x_sample.shape
    _, n_pool, page, kv_heads, head_dim = cache_k.shape
    idx_dim = cache_idx_k.shape[-1]
    _, _, ret_heads, ret_dk, ret_dv = state_ret.shape
    _, _, mem_tokens, mem_heads, mem_dim = cache_mem_k.shape
    q_heads = d // head_dim
    group = q_heads // kv_heads
    kvw = kv_heads * head_dim
    n_pages = page_table.shape[1]
    past = n_pages * page
    dims = dict(idx_dim=idx_dim, ret_dk=ret_dk, head_dim=head_dim, mem_dim=mem_dim)
    lay, _ = _proj_layout(d, kvw, idx_dim, ret_heads * ret_dk, ret_heads * ret_dv, mem_heads * mem_dim,
                          w_in.shape[2] - (d + 2 * kvw + IDX_HEADS * idx_dim + idx_dim + IDX_HEADS
                                           + 2 * ret_heads * ret_dk + 2 * ret_heads * ret_dv + mem_heads * mem_dim))
    o_ki = d + 2 * kvw + IDX_HEADS * idx_dim
    o_qr = o_ki + idx_dim + IDX_HEADS

    cos_p, sin_p = _rotary_tables(jnp.arange(s), ret_dk, min(PROJ_ROWS, b * s))
    cos_s, sin_s = _rotary_tables(past + jnp.arange(t), ret_dk, min(PROJ_ROWS, db * t))
    tri = (jnp.arange(TRI_W)[:, None] <= jnp.arange(TRI_W)[None, :]).astype(BF16)
    topk_p = min(TOPK_MAX, s // 4)
    topk_s = min(TOPK_MAX, (past + t) // 4)

    xp = x_prompt.reshape(b * s, d)
    xs = x_sample.reshape(db * t, d)
    outs = {k: [] for k in ("k_p", "v_p", "ik_p", "rs_p", "mk_p", "mv_p", "k_s", "v_s", "ik_s", "rs_s")}
    for l in range(depth):
        last = l == depth - 1
        wl = w_in[l]
        w_packed = jnp.concatenate(
            [wl[:, :o_ki], wl[:, o_ki:o_qr], jnp.zeros((d, LANES - (o_qr - o_ki)), wl.dtype), wl[:, o_qr:]],
            axis=1).astype(BF16)
        w_out_b, w_gate_b, w_up_b, w_down_b = (a[l].astype(BF16) for a in (w_out, w_gate, w_up, w_down))

        p = _project(xp, attn_norm_g[l], w_packed, cos_p, sin_p, lay, dims, BF16)
        qi = p["qi"].reshape(b * s, IDX_HEADS, idx_dim).transpose(1, 0, 2)
        qa = p["qa"].reshape(b * s, q_heads, head_dim).transpose(1, 0, 2)
        kit = p["ki"].astype(BF16).reshape(b, s, idx_dim).transpose(0, 2, 1)
        kt = p["ka"].astype(BF16).reshape(b, s, kvw).transpose(0, 2, 1)
        va = p["va"].astype(BF16).reshape(b, s, kv_heads, head_dim)
        vb = jnp.concatenate([va, jnp.ones((b, s, kv_heads, LANES - head_dim), BF16)], axis=-1)
        o_a = _sparse_attention_prompt(qi, p["wi"], qa, kit, kt, vb.reshape(b, s, kv_heads * LANES), tri, topk_p,
                                       kv_heads, head_dim)
        o_r, st_p = _retention_prompt(p["qr"].reshape(b, s, -1), p["kr"].reshape(b, s, -1),
                                      p["vr"].reshape(b, s, -1), p["gr"].reshape(b, s, -1), ret_gn_g[l],
                                      ret_heads, ret_dk, ret_dv)
        mk, mv = _memory_kv(mem_prompt.reshape(b * mem_tokens, d), mem_norm_g[l], w_mem_kv[l].astype(BF16))
        o_m = _memory_attention_prompt(p["qm"].reshape(b, s, -1), mk.astype(BF16).reshape(b, mem_tokens, -1),
                                       mv.astype(BF16).reshape(b, mem_tokens, -1), mem_heads, mem_dim)
        xp = _merge_ffn(xp, p["gt"], o_a, o_r.reshape(b * s, -1), o_m.reshape(b * s, -1), w_out_b, ffn_norm_g[l],
                        w_gate_b, w_up_b, w_down_b, final_norm_g, last)
        outs["k_p"].append(p["ka"].reshape(b, s, kv_heads, head_dim))
        outs["v_p"].append(p["va"].reshape(b, s, kv_heads, head_dim))
        outs["ik_p"].append(p["ki"].reshape(b, s, idx_dim))
        outs["rs_p"].append(st_p)
        outs["mk_p"].append(mk.reshape(b, mem_tokens, mem_heads, mem_dim))
        outs["mv_p"].append(mv.reshape(b, mem_tokens, mem_heads, mem_dim))

        q = _project(xs, attn_norm_g[l], w_packed, cos_s, sin_s, lay, dims, F32)
        qi_s = q["qi"].reshape(db, t, IDX_HEADS, idx_dim).transpose(0, 2, 1, 3).reshape(db, IDX_HEADS * t, idx_dim)
        qa_s = q["qa"].reshape(db, t, kv_heads, group, head_dim).transpose(0, 2, 3, 1, 4)
        qbd = (qa_s.reshape(db, kv_heads, group * t, 1, head_dim)
               * jnp.eye(kv_heads, dtype=F32)[None, :, None, :, None]).reshape(db, kv_heads * group * t, kvw)
        pad_rows = lambda a: jnp.pad(a.reshape(db, t, -1), ((0, 0), (0, LANES - t), (0, 0))).astype(BF16)
        o_a = _sparse_attention_sample(
            page_table, qi_s.astype(BF16), q["wi"].reshape(db, t, IDX_HEADS), qbd.astype(BF16),
            pad_rows(q["ki"]), pad_rows(q["ka"]), pad_rows(q["va"]), tri,
            jnp.swapaxes(cache_idx_k[l], 1, 2),
            cache_k[l].transpose(0, 2, 3, 1).reshape(n_pool, kvw, page),
            cache_v[l].transpose(0, 2, 3, 1).reshape(n_pool, kvw, page),
            topk_s, kv_heads, group, head_dim).reshape(db * t, d)
        o_r, st_s = _retention_sample(state_ret[l], q["qr"], q["kr"], q["vr"], q["gr"], ret_gn_g[l],
                                      ret_heads, ret_dk, ret_dv, t)
        o_m = _memory_attention_sample(q["qm"], cache_mem_k[l], cache_mem_v[l], t)
        xs = _merge_ffn(xs, q["gt"], o_a, o_r, o_m, w_out_b, ffn_norm_g[l], w_gate_b, w_up_b, w_down_b,
                        final_norm_g, last)
        outs["k_s"].append(q["ka"].reshape(db, t, kv_heads, head_dim))
        outs["v_s"].append(q["va"].reshape(db, t, kv_heads, head_dim))
        outs["ik_s"].append(q["ki"].reshape(db, t, idx_dim))
        outs["rs_s"].append(st_s)

    stk = lambda k: jnp.stack(outs[k])
    return (xp.reshape(b, s, d), xs.reshape(db, t, d), stk("k_p"), stk("v_p"), stk("ik_p"), stk("rs_p"),
            stk("mk_p"), stk("mv_p"), stk("k_s"), stk("v_s"), stk("ik_s"), stk("rs_s"))
```

```python
import functools
import math

import numpy as np
import jax
import jax.numpy as jnp
from jax import lax
from jax.experimental import pallas as pl
from jax.experimental.pallas import tpu as pltpu

F32 = jnp.float32
BF16 = jnp.bfloat16
I32 = jnp.int32

IDX_HEADS = 4
TOPK_MAX = 256
RET_CHUNK = 128
RET_DECAY_OFFSET = 5.0
ROPE_BASE = 10000.0
EPS = 1e-6

LANES = 128
VMEM_LIMIT_BYTES = 56 * 1024 * 1024

NEG = -1e30
INT_MIN = -(2 ** 31)
LOG2E = math.log2(math.e)
MIN_NORMAL = 2.0 ** -126

PROJ_ROWS = 256
ATTN_TQ = 128
ATTN_KB = 1024
SAMPLE_KB = 1024
COUNT_ELEMS = 65536
TRI_W = 256
PREFILTER_DEPTH = 12
PREFILTER_MIN_CHUNKS = 8
MEM_ROWS = 512
SAMPLE_GROUP = 8


def _params(n_axes=1):
    return pltpu.CompilerParams(dimension_semantics=("arbitrary",) * n_axes,
                                vmem_limit_bytes=VMEM_LIMIT_BYTES)


def _whole_vmem():
    return pl.BlockSpec(memory_space=pltpu.VMEM)


def _rmsnorm(x, g):
    return (x * lax.rsqrt(jnp.mean(x * x, axis=-1, keepdims=True) + EPS)) * g


def _sigmoid(x):
    return 1.0 / (1.0 + jnp.exp(-x))


def _dot(a, b):
    return jnp.dot(a, b, preferred_element_type=F32)


def _dot_nt(a, b):
    return lax.dot_general(a, b, (((1,), (1,)), ((), ())), preferred_element_type=F32)


def _proj_layout(d_model, kv_w, idx_dim, ret_qk_w, ret_v_w, mem_w, n_gate):
    segs = [("qa", d_model), ("ka", kv_w), ("va", kv_w), ("qi", IDX_HEADS * idx_dim), ("kw", LANES),
            ("qr", ret_qk_w), ("kr", ret_qk_w), ("vr", ret_v_w), ("gr", ret_v_w), ("qm", mem_w),
            ("gt", n_gate)]
    out, off = {}, 0
    for name, w in segs:
        out[name] = (off, off + w)
        off += w
    return out, off


def _proj_kernel(x_ref, g_ref, w_ref, cos_ref, sin_ref,
                 qa_ref, ka_ref, va_ref, qi_ref, ki_ref, wi_ref, qr_ref, kr_ref, vr_ref, gr_ref, qm_ref, gt_ref,
                 *, lay, idx_dim, ret_dk, qa_scale, wi_scale, kr_scale, qm_scale):
    hb = _rmsnorm(x_ref[...], g_ref[...]).astype(BF16)

    def seg(name):
        lo, hi = lay[name]
        return _dot(hb, w_ref[:, lo:hi])

    qa_ref[...] = (seg("qa") * qa_scale).astype(qa_ref.dtype)
    ka_ref[...] = seg("ka")
    va_ref[...] = seg("va")
    qi_ref[...] = seg("qi").astype(qi_ref.dtype)
    kw = seg("kw")
    ki_ref[...] = kw[:, :idx_dim]
    wi_ref[...] = kw[:, idx_dim:idx_dim + IDX_HEADS] * wi_scale

    cos2 = cos_ref[...]
    sin2 = sin_ref[...]

    def rotary(z, out_ref, scale):
        for h in range(z.shape[1] // ret_dk):
            zh = z[:, h * ret_dk:(h + 1) * ret_dk]
            r = zh * cos2 + pltpu.roll(zh, ret_dk // 2, 1) * sin2
            out_ref[:, h * ret_dk:(h + 1) * ret_dk] = r if scale is None else r * scale

    rotary(seg("qr"), qr_ref, None)
    rotary(seg("kr"), kr_ref, kr_scale)
    vr_ref[...] = seg("vr")
    gr_ref[...] = seg("gr")
    qm_ref[...] = (seg("qm") * qm_scale).astype(qm_ref.dtype)
    gt_ref[...] = _sigmoid(seg("gt"))


def _project(x2d, g, w_packed, cos2, sin2, lay, dims, q_dtype):
    t, d = x2d.shape
    tm = min(PROJ_ROWS, t)
    n_pos_blocks = cos2.shape[0] // tm
    widths = {k: hi - lo for k, (lo, hi) in lay.items()}
    out_defs = [("qa", widths["qa"], q_dtype), ("ka", widths["ka"], F32), ("va", widths["va"], F32),
                ("qi", widths["qi"], q_dtype), ("ki", dims["idx_dim"], F32), ("wi", IDX_HEADS, F32),
                ("qr", widths["qr"], F32), ("kr", widths["kr"], F32), ("vr", widths["vr"], F32),
                ("gr", widths["gr"], F32), ("qm", widths["qm"], q_dtype), ("gt", widths["gt"], F32)]
    row = lambda w: pl.BlockSpec((tm, w), lambda i: (i, 0))
    kern = functools.partial(
        _proj_kernel, lay=lay, idx_dim=dims["idx_dim"], ret_dk=dims["ret_dk"],
        qa_scale=dims["head_dim"] ** -0.5 * LOG2E, wi_scale=(IDX_HEADS ** -0.5) * (dims["idx_dim"] ** -0.5),
        kr_scale=dims["ret_dk"] ** -0.5, qm_scale=dims["mem_dim"] ** -0.5)
    outs = pl.pallas_call(
        kern,
        grid=(t // tm,),
        in_specs=[row(d), pl.BlockSpec((1, d), lambda i: (0, 0)), _whole_vmem(),
                  pl.BlockSpec((tm, LANES), lambda i: (i % n_pos_blocks, 0)),
                  pl.BlockSpec((tm, LANES), lambda i: (i % n_pos_blocks, 0))],
        out_specs=[row(w) for _, w, _ in out_defs],
        out_shape=[jax.ShapeDtypeStruct((t, w), dt) for _, w, dt in out_defs],
        compiler_params=_params(),
        name="proj",
    )(x2d, g.reshape(1, d), w_packed, cos2, sin2)
    return {name: o for (name, _, _), o in zip(out_defs, outs)}


def _rotary_tables(pos, ret_dk, rows):
    half = ret_dk // 2
    inv = ROPE_BASE ** (-jnp.arange(half, dtype=F32) / half)
    ang = pos.astype(F32)[:, None] * inv[None, :]
    cos, sin = jnp.cos(ang), jnp.sin(ang)
    cos2 = jnp.concatenate([cos, cos], axis=1)
    sin2 = jnp.concatenate([-sin, sin], axis=1)
    reps = max(1, rows // cos2.shape[0])
    return jnp.tile(cos2, (reps, 1)), jnp.tile(sin2, (reps, 1))


def _index_scores(s4, w):
    acc = w[:, 0:1] * jnp.maximum(s4[0], 0.0)
    for h in range(1, IDX_HEADS):
        acc = acc + w[:, h:h + 1] * jnp.maximum(s4[h], 0.0)
    return jnp.where(jnp.abs(acc) < MIN_NORMAL, 0.0, acc)


def _sortable_key(bits):
    return jnp.where(bits >= 0, bits, bits ^ 0x7FFFFFFF)


def _count_chunk(rows):
    return max(LANES, COUNT_ELEMS // rows)


def _tree_sum(parts):
    while len(parts) > 1:
        parts = [a + b for a, b in zip(parts[::2], parts[1::2])] + parts[len(parts) & ~1:]
    return parts[0]


def _count_ge(keys_ref, rows, n_chunks, cand):
    cw = _count_chunk(rows)
    cb = jnp.broadcast_to(cand, (rows, LANES))

    def body(c, a):
        blk = keys_ref[:, pl.ds(pl.multiple_of(c * cw, cw), cw)]
        return a + _tree_sum([jnp.where(blk[:, u * LANES:(u + 1) * LANES] >= cb, 1.0, 0.0)
                              for u in range(cw // LANES)])

    a = lax.fori_loop(0, n_chunks, body, jnp.zeros((rows, LANES), F32), unroll=isinstance(n_chunks, int))
    return jnp.sum(a, axis=1, keepdims=True)


def _kth_largest_key(keys_ref, rows, n_chunks, topk):
    def bit_step(b, u):
        cand_u = u | lax.shift_left(jnp.int32(1), 31 - b)
        return jnp.where(_count_ge(keys_ref, rows, n_chunks, cand_u ^ INT_MIN) >= topk, cand_u, u)

    return lax.fori_loop(0, 32, bit_step, jnp.zeros((rows, 1), I32)) ^ INT_MIN


def _tie_quota(thr, n_greater, topk):
    return jnp.where(thr == INT_MIN, 0.0, topk - n_greater)


def _topk_threshold(keys_ref, rows, n_chunks, topk):
    thr = _kth_largest_key(keys_ref, rows, n_chunks, topk)
    return thr, _tie_quota(thr, _count_ge(keys_ref, rows, n_chunks, thr + 1), topk)


def _lane_top_insert(cand_ref, sc_ref, rows, width):
    depth = cand_ref.shape[0]

    def body(g, c):
        r = pl.ds(pl.multiple_of(g * 8, 8), 8)
        tops = [cand_ref[m, r, :] for m in range(depth)]
        for c0 in range(0, width, LANES):
            x = sc_ref[r, c0:c0 + LANES]
            for m in range(depth):
                tops[m], x = jnp.maximum(tops[m], x), jnp.minimum(tops[m], x)
        for m in range(depth):
            cand_ref[m, r, :] = tops[m]
        return c

    lax.fori_loop(0, rows // 8, body, 0)


def _selection_bias(keys, thr, need, run, tri):
    tw = tri.shape[0]
    eq = keys == thr
    gt = keys > thr
    eq_b = jnp.where(eq, 1.0, 0.0).astype(BF16)
    parts = []
    for c in range(keys.shape[1] // tw):
        pre = _dot(eq_b[:, c * tw:(c + 1) * tw], tri) + run
        run = pre[:, tw - 1:]
        parts.append(pre)
    prefix = parts[0] if len(parts) == 1 else jnp.concatenate(parts, axis=1)
    bias = jnp.where(gt, 0.0, jnp.where(eq, jnp.where(prefix <= need, 0.0, NEG), NEG))
    return bias, run


def _lane_tile(x, width):
    reps = width // x.shape[1]
    return x if reps == 1 else jnp.concatenate([x] * reps, axis=1)


def _softmax_tile(s, pv, m_ref, l_ref, acc_ref, idx):
    m_prev = m_ref[idx]
    m_new = jnp.maximum(m_prev, jnp.max(s, axis=1, keepdims=True))
    alpha = jnp.exp2(m_prev - m_new)
    p = jnp.exp2(s - _lane_tile(m_new, s.shape[1]))
    l_ref[idx] = alpha * l_ref[idx] + jnp.sum(p, axis=1, keepdims=True)
    acc = acc_ref[idx]
    acc_ref[idx] = _lane_tile(alpha, acc.shape[1]) * acc + pv(p.astype(BF16))
    m_ref[idx] = m_new


def _sparse_prompt_kernel(qi_ref, wi_ref, qa_ref, kit_ref, kt_ref, v_ref, tri_ref, o_ref,
                          keys_ref, sc_ref, cand_ref, ckeys_ref, p_ref, m_ref, acc_ref,
                          *, tq, kb, topk, kv_heads, group, head_dim):
    i = pl.program_id(1)
    n_tiles = ((i + 1) * tq + kb - 1) // kb
    row_pos = i * tq + lax.broadcasted_iota(I32, (tq, kb), 0)
    col_iota = lax.broadcasted_iota(I32, (tq, kb), 1)
    w = wi_ref[...]
    qi = qi_ref[...].reshape(IDX_HEADS * tq, qi_ref.shape[2])
    cw = _count_chunk(tq)
    n_chunks = n_tiles * (kb // cw)
    depth = cand_ref.shape[0]
    cand_chunks = depth * LANES // cw
    prefilter = n_chunks >= PREFILTER_MIN_CHUNKS

    @pl.when(prefilter)
    def _():
        cand_ref[...] = jnp.full(cand_ref.shape, -jnp.inf, F32)

    def score_tile(j, c):
        k0 = pl.multiple_of(j * kb, kb)
        s4 = _dot(qi, kit_ref[0, :, pl.ds(k0, kb)]).reshape(IDX_HEADS, tq, kb)
        sc = _index_scores(s4, w)
        visible = k0 + col_iota <= row_pos
        keys_ref[:, pl.ds(k0, kb)] = jnp.where(visible, _sortable_key(lax.bitcast_convert_type(sc, I32)), INT_MIN)

        @pl.when(prefilter)
        def _():
            sc_ref[...] = jnp.where(visible, sc, -jnp.inf)
            _lane_top_insert(cand_ref, sc_ref, tq, kb)

        return c

    lax.fori_loop(0, n_tiles, score_tile, 0)

    def ranked_directly():
        return _topk_threshold(keys_ref, tq, n_chunks, topk)

    def ranked_from_lane_tops():
        for m in range(depth):
            ckeys_ref[:, m * LANES:(m + 1) * LANES] = _sortable_key(lax.bitcast_convert_type(cand_ref[m], I32))
        thr = _kth_largest_key(ckeys_ref, tq, cand_chunks, topk)
        n_greater = _count_ge(keys_ref, tq, n_chunks, thr + 1)
        kept = _count_ge(ckeys_ref, tq, cand_chunks, thr + 1)
        missed = jnp.max(n_greater - kept)
        return lax.cond(missed == 0.0, lambda: (thr, _tie_quota(thr, n_greater, topk)), ranked_directly)

    thr, need = lax.cond(prefilter, ranked_from_lane_tops, ranked_directly)

    m_ref[...] = jnp.full(m_ref.shape, NEG, F32)
    acc_ref[...] = jnp.zeros(acc_ref.shape, F32)

    def attn_tile(j, run):
        k0 = pl.multiple_of(j * kb, kb)
        bias, run = _selection_bias(keys_ref[:, pl.ds(k0, kb)], thr, need, run, tri_ref[...])
        gr = group * tq
        for pair in range(kv_heads // 2):
            alphas = []
            for half in range(2):
                n = 2 * pair + half
                qn = qa_ref[n * group:(n + 1) * group].reshape(gr, head_dim)
                s = _dot(qn, kt_ref[0, n * head_dim:(n + 1) * head_dim, pl.ds(k0, kb)])
                s = (s.reshape(group, tq, kb) + bias[None]).reshape(gr, kb)
                m_prev = m_ref[n]
                m_new = jnp.maximum(m_prev, jnp.max(s, axis=1, keepdims=True))
                alphas.append(jnp.exp2(m_prev - m_new))
                p_ref[half * gr:(half + 1) * gr, :] = jnp.exp2(s - _lane_tile(m_new, kb)).astype(BF16)
                m_ref[n] = m_new
            pv = _dot(p_ref[...], v_ref[0, pl.ds(k0, kb), pair * 2 * LANES:(pair + 1) * 2 * LANES])
            for half in range(2):
                n = 2 * pair + half
                acc_ref[n] = alphas[half] * acc_ref[n] + pv[half * gr:(half + 1) * gr, half * LANES:(half + 1) * LANES]
        return run

    lax.fori_loop(0, n_tiles, attn_tile, jnp.zeros((tq, 1), F32))

    for n in range(kv_heads):
        acc = acc_ref[n]
        o = acc * pltpu.roll(1.0 / acc, LANES - head_dim, 1)
        for g in range(group):
            h = n * group + g
            o_ref[:, h * head_dim:(h + 1) * head_dim] = o[g * tq:(g + 1) * tq, :head_dim]


def _sparse_attention_prompt(qi, wi, qa, kit, kt, v, tri, topk, kv_heads, head_dim):
    q_heads = qa.shape[0]
    b, s, _ = v.shape
    tq, kb = min(ATTN_TQ, s), min(ATTN_KB, s)
    nq = s // tq
    group = q_heads // kv_heads
    kern = functools.partial(_sparse_prompt_kernel, tq=tq, kb=kb, topk=topk, kv_heads=kv_heads,
                             group=group, head_dim=head_dim)
    resident = lambda shape: pl.BlockSpec((1,) + shape, lambda bi, i: (bi, 0, 0), pipeline_mode=pl.Buffered(1))
    return pl.pallas_call(
        kern,
        grid=(b, nq),
        in_specs=[pl.BlockSpec((IDX_HEADS, tq, qi.shape[2]), lambda bi, i: (0, bi * nq + i, 0)),
                  pl.BlockSpec((tq, IDX_HEADS), lambda bi, i: (bi * nq + i, 0)),
                  pl.BlockSpec((q_heads, tq, head_dim), lambda bi, i: (0, bi * nq + i, 0)),
                  resident(kit.shape[1:]), resident(kt.shape[1:]), resident(v.shape[1:]), _whole_vmem()],
        out_specs=pl.BlockSpec((tq, q_heads * head_dim), lambda bi, i: (bi * nq + i, 0)),
        out_shape=jax.ShapeDtypeStruct((b * s, q_heads * head_dim), F32),
        scratch_shapes=[pltpu.VMEM((tq, s), I32),
                        pltpu.VMEM((tq, kb), F32),
                        pltpu.VMEM((PREFILTER_DEPTH, tq, LANES), F32),
                        pltpu.VMEM((tq, PREFILTER_DEPTH * LANES), I32),
                        pltpu.VMEM((2 * group * tq, kb), BF16),
                        pltpu.VMEM((kv_heads, group * tq, LANES), F32),
                        pltpu.VMEM((kv_heads, group * tq, LANES), F32)],
        compiler_params=_params(2),
        name="sparse_attn_prompt",
    )(qi, wi, qa, kit, kt, v, tri)


def _sparse_sample_kernel(pt_ref, qi_ref, wi_ref, qbd_ref, kin_ref, kan_ref, van_ref, tri_ref,
                          cidx_hbm, ck_hbm, cv_hbm, o_ref,
                          ibuf, kbuf, vbuf, sems, keys_ref, m_ref, l_ref, acc_ref,
                          *, t, kb, topk, n_pages, page, kv_heads, group, head_dim):
    b = pl.program_id(0)
    nb = pl.num_programs(0)
    past = n_pages * page
    rows = kv_heads * group * t
    cw = _count_chunk(t)

    def page_copies(bb, slot, p):
        pg = pt_ref[bb, p]
        dst = pl.ds(pl.multiple_of(p * page, page), page)
        return (pltpu.make_async_copy(cidx_hbm.at[pg], ibuf.at[slot, :, dst], sems.at[slot, 0]),
                pltpu.make_async_copy(ck_hbm.at[pg], kbuf.at[slot, :, dst], sems.at[slot, 1]),
                pltpu.make_async_copy(cv_hbm.at[pg], vbuf.at[slot, :, dst], sems.at[slot, 2]))

    def start_fetch(bb, slot):
        def body(p, c):
            for cp in page_copies(bb, slot, p):
                cp.start()
            return c
        lax.fori_loop(0, n_pages, body, 0)

    def wait_fetch(bb, slot):
        def body(p, c):
            for cp in page_copies(bb, slot, p):
                cp.wait()
            return c
        lax.fori_loop(0, n_pages, body, 0)

    slot = b % 2

    @pl.when(b == 0)
    def _():
        start_fetch(b, slot)

    @pl.when(b + 1 < nb)
    def _():
        start_fetch(b + 1, 1 - slot)

    wait_fetch(b, slot)

    w = wi_ref[0]
    qi = qi_ref[0]
    n_tiles = past // kb

    def score_tile(j, c):
        k0 = pl.multiple_of(j * kb, kb)
        s4 = _dot(qi, ibuf[slot, :, pl.ds(k0, kb)].astype(BF16)).reshape(IDX_HEADS, t, kb)
        keys_ref[:, pl.ds(k0, kb)] = _sortable_key(lax.bitcast_convert_type(_index_scores(s4, w), I32))
        return c

    lax.fori_loop(0, n_tiles, score_tile, 0)
    s4 = _dot_nt(qi, kin_ref[0]).reshape(IDX_HEADS, t, LANES)
    key_new = _sortable_key(lax.bitcast_convert_type(_index_scores(s4, w), I32))
    new_col = lax.broadcasted_iota(I32, (t, LANES), 1)
    new_row = lax.broadcasted_iota(I32, (t, LANES), 0)
    keys_ref[:, past:past + LANES] = jnp.where(new_col <= new_row, key_new, INT_MIN)
    keys_ref[:, past + LANES:past + cw] = jnp.full((t, cw - LANES), INT_MIN, I32)

    thr, need = _topk_threshold(keys_ref, t, past // cw + 1, topk)

    m_ref[...] = jnp.full(m_ref.shape, NEG, F32)
    l_ref[...] = jnp.zeros(l_ref.shape, F32)
    acc_ref[...] = jnp.zeros(acc_ref.shape, F32)
    qbd = qbd_ref[0]

    def tile_bias(bias, width):
        return jnp.broadcast_to(bias[None], (rows // t, t, width)).reshape(rows, width)

    def attn_tile(j, run):
        k0 = pl.multiple_of(j * kb, kb)
        bias, run = _selection_bias(keys_ref[:, pl.ds(k0, kb)], thr, need, run, tri_ref[...])
        s = _dot(qbd, kbuf[slot, :, pl.ds(k0, kb)].astype(BF16)) + tile_bias(bias, kb)
        vt = vbuf[slot, :, pl.ds(k0, kb)].astype(BF16)
        _softmax_tile(s, lambda p: _dot_nt(p, vt), m_ref, l_ref, acc_ref, 0)
        return run

    run = lax.fori_loop(0, n_tiles, attn_tile, jnp.zeros((t, 1), F32))
    bias, _ = _selection_bias(keys_ref[:, past:past + LANES], thr, need, run, tri_ref[:LANES, :LANES])
    s = _dot_nt(qbd, kan_ref[0]) + tile_bias(bias, LANES)
    van = van_ref[0]
    _softmax_tile(s, lambda p: _dot(p, van), m_ref, l_ref, acc_ref, 0)

    o = acc_ref[0] * (1.0 / _lane_tile(l_ref[0], acc_ref.shape[2]))
    for n in range(kv_heads):
        for g in range(group):
            h = n * group + g
            r0 = (n * group + g) * t
            o_ref[0, :, h * head_dim:(h + 1) * head_dim] = o[r0:r0 + t, n * head_dim:(n + 1) * head_dim]


def _sparse_attention_sample(page_table, qi, wi, qbd, ki_new, ka_new, va_new, tri, cache_idx_t, cache_k_t, cache_v_t,
                             topk, kv_heads, group, head_dim):
    db, t, _ = wi.shape
    n_pages = page_table.shape[1]
    page = cache_idx_t.shape[2]
    past = n_pages * page
    kb = min(SAMPLE_KB, past)
    rows = kv_heads * group * t
    kvw = kv_heads * head_dim
    cw = _count_chunk(t)
    kern = functools.partial(_sparse_sample_kernel, t=t, kb=kb, topk=topk, n_pages=n_pages, page=page,
                             kv_heads=kv_heads, group=group, head_dim=head_dim)
    per_b = lambda shape: pl.BlockSpec((1,) + shape, lambda b, pt: (b, 0, 0))
    grid_spec = pltpu.PrefetchScalarGridSpec(
        num_scalar_prefetch=1,
        grid=(db,),
        in_specs=[per_b(qi.shape[1:]), per_b(wi.shape[1:]), per_b(qbd.shape[1:]), per_b(ki_new.shape[1:]),
                  per_b(ka_new.shape[1:]), per_b(va_new.shape[1:]),
                  pl.BlockSpec(tri.shape, lambda b, pt: (0, 0)),
                  pl.BlockSpec(memory_space=pl.ANY), pl.BlockSpec(memory_space=pl.ANY),
                  pl.BlockSpec(memory_space=pl.ANY)],
        out_specs=per_b((t, kv_heads * group * head_dim)),
        scratch_shapes=[pltpu.VMEM((2, cache_idx_t.shape[1], past), F32),
                        pltpu.VMEM((2, kvw, past), F32),
                        pltpu.VMEM((2, kvw, past), F32),
                        pltpu.SemaphoreType.DMA((2, 3)),
                        pltpu.VMEM((t, past + cw), I32),
                        pltpu.VMEM((1, rows, LANES), F32),
                        pltpu.VMEM((1, rows, LANES), F32),
                        pltpu.VMEM((1, rows, kvw), F32)])
    return pl.pallas_call(
        kern,
        grid_spec=grid_spec,
        out_shape=jax.ShapeDtypeStruct((db, t, kv_heads * group * head_dim), F32),
        compiler_params=_params(),
        name="sparse_attn_sample",
    )(page_table, qi, wi, qbd, ki_new, ka_new, va_new, tri, cache_idx_t, cache_k_t, cache_v_t)


def _log_gamma(heads):
    return np.log(1.0 - np.exp2(-RET_DECAY_OFFSET - np.arange(heads, dtype=np.float64)))


def _retention_tables(heads, chunk, dk, group):
    lg = _log_gamma(heads)
    i = np.arange(chunk, dtype=np.float64)
    diff = i[:, None] - i[None, :]
    decay = np.where(diff >= 0, np.exp(lg[:, None, None] * np.maximum(diff, 0.0)), 0.0)
    decay_bd = np.zeros((heads, group * chunk, group * chunk))
    for gidx in range(group):
        decay_bd[:, gidx * chunk:(gidx + 1) * chunk, gidx * chunk:(gidx + 1) * chunk] = decay
    q_scale = np.tile(np.exp(lg[:, None] * (i + 1.0)), (1, group))
    k_scale = np.tile(np.exp(lg[:, None] * (chunk - 1.0 - i)), (1, group))
    state_scale = np.exp(lg * chunk)
    rows_q = np.broadcast_to(q_scale[:, :, None], (heads, group * chunk, dk))
    rows_k = np.broadcast_to(k_scale[:, :, None], (heads, group * chunk, dk))
    cols_k = np.broadcast_to(k_scale[:, None, :], (heads, dk, group * chunk))
    as32 = lambda a: jnp.asarray(np.ascontiguousarray(a), F32)
    return as32(decay_bd), as32(rows_q), as32(rows_k), as32(cols_k), [float(v) for v in state_scale]


def _groupnorm_gate(o, gn_g, g_r):
    mu = jnp.mean(o, axis=-1, keepdims=True)
    d = o - mu
    var = jnp.mean(d * d, axis=-1, keepdims=True)
    return (g_r * _sigmoid(g_r)) * ((d * lax.rsqrt(var + EPS)) * gn_g)


def _retention_prompt_kernel(q_ref, k_ref, v_ref, g_ref, gn_ref, dec_ref, qs_ref, ks_ref, o_ref, st_ref, state,
                             *, heads, dk, dv, state_scale):
    c = pl.program_id(1)

    @pl.when(c == 0)
    def _():
        state[...] = jnp.zeros(state.shape, F32)

    for h in range(heads):
        q = q_ref[0, :, h * dk:(h + 1) * dk]
        k = k_ref[0, :, h * dk:(h + 1) * dk]
        v = v_ref[0, :, h * dv:(h + 1) * dv].astype(BF16)
        s_old = state[h]
        inner = _dot_nt(q.astype(BF16), k.astype(BF16)) * dec_ref[h]
        o = _dot(inner.astype(BF16), v) + _dot((q * qs_ref[h]).astype(BF16), s_old.astype(BF16))
        state[h] = state_scale[h] * s_old + _dot((k * ks_ref[h]).T.astype(BF16), v)
        o_ref[0, :, h * dv:(h + 1) * dv] = _groupnorm_gate(o, gn_ref[:, h * dv:(h + 1) * dv],
                                                           g_ref[0, :, h * dv:(h + 1) * dv])

    @pl.when(c == pl.num_programs(1) - 1)
    def _():
        st_ref[0] = state[...]


def _retention_prompt(q, k, v, g, gn_g, heads, dk, dv):
    b, s, _ = q.shape
    chunk = min(RET_CHUNK, s)
    dec, qs, ks, _, state_scale = _retention_tables(heads, chunk, dk, 1)
    blk = lambda w: pl.BlockSpec((1, chunk, w), lambda bi, ci: (bi, ci, 0))
    const = lambda a: pl.BlockSpec(a.shape, lambda bi, ci: (0,) * a.ndim)
    gn2 = gn_g.reshape(1, heads * dv)
    kern = functools.partial(_retention_prompt_kernel, heads=heads, dk=dk, dv=dv, state_scale=state_scale)
    return pl.pallas_call(
        kern,
        grid=(b, s // chunk),
        in_specs=[blk(heads * dk), blk(heads * dk), blk(heads * dv), blk(heads * dv),
                  const(gn2), const(dec), const(qs), const(ks)],
        out_specs=[blk(heads * dv), pl.BlockSpec((1, heads, dk, dv), lambda bi, ci: (bi, 0, 0, 0))],
        out_shape=[jax.ShapeDtypeStruct((b, s, heads * dv), F32),
                   jax.ShapeDtypeStruct((b, heads, dk, dv), F32)],
        scratch_shapes=[pltpu.VMEM((heads, dk, dv), F32)],
        compiler_params=_params(2),
        name="retention_prompt",
    )(q, k, v, g, gn2, dec, qs, ks)


def _retention_sample_kernel(q_ref, k_ref, kt_ref, v_ref, g_ref, gn_ref, dec_ref, qs_ref, kst_ref, st_ref,
                             o_ref, sto_ref, *, heads, dk, dv, t, group, state_scale):
    rows = group * t
    col_batch = lax.broadcasted_iota(I32, (dk, rows), 1) // t
    for h in range(heads):
        q = q_ref[:, h * dk:(h + 1) * dk]
        k = k_ref[:, h * dk:(h + 1) * dk]
        v = v_ref[:, h * dv:(h + 1) * dv].astype(BF16)
        inner = _dot_nt(q.astype(BF16), k.astype(BF16)) * dec_ref[h]
        o = _dot(inner.astype(BF16), v)
        qd = (q * qs_ref[h]).astype(BF16)
        kdt = kt_ref[0, h] * kst_ref[h]
        cross = []
        for bi in range(group):
            s_old = st_ref[bi, h]
            cross.append(_dot(qd, s_old.astype(BF16))[bi * t:(bi + 1) * t])
            kd_b = jnp.where(col_batch == bi, kdt, 0.0).astype(BF16)
            sto_ref[bi, h] = state_scale[h] * s_old + _dot(kd_b, v)
        o = o + jnp.concatenate(cross, axis=0)
        o_ref[:, h * dv:(h + 1) * dv] = _groupnorm_gate(o, gn_ref[:, h * dv:(h + 1) * dv],
                                                        g_ref[:, h * dv:(h + 1) * dv])


def _retention_sample(state, q, k, v, g, gn_g, heads, dk, dv, t):
    db = state.shape[0]
    group = min(SAMPLE_GROUP, db)
    rows = group * t
    dec, qs, _, kst, state_scale = _retention_tables(heads, t, dk, group)
    kt = k.reshape(db // group, rows, heads, dk).transpose(0, 2, 3, 1)
    gn2 = gn_g.reshape(1, heads * dv)
    blk = lambda w: pl.BlockSpec((rows, w), lambda i: (i, 0))
    const = lambda a: pl.BlockSpec(a.shape, lambda i: (0,) * a.ndim)
    st_spec = pl.BlockSpec((group, heads, dk, dv), lambda i: (i, 0, 0, 0))
    kern = functools.partial(_retention_sample_kernel, heads=heads, dk=dk, dv=dv, t=t, group=group,
                             state_scale=state_scale)
    return pl.pallas_call(
        kern,
        grid=(db // group,),
        in_specs=[blk(heads * dk), blk(heads * dk),
                  pl.BlockSpec((1, heads, dk, rows), lambda i: (i, 0, 0, 0)),
                  blk(heads * dv), blk(heads * dv), const(gn2), const(dec), const(qs), const(kst), st_spec],
        out_specs=[blk(heads * dv), st_spec],
        out_shape=[jax.ShapeDtypeStruct((db * t, heads * dv), F32),
                   jax.ShapeDtypeStruct(state.shape, F32)],
        compiler_params=_params(),
        name="retention_sample",
    )(q, k, kt, v, g, gn2, dec, qs, kst, state)


def _mem_kv_kernel(x_ref, g_ref, w_ref, k_ref, v_ref):
    hb = _rmsnorm(x_ref[...], g_ref[...]).astype(BF16)
    half = w_ref.shape[1] // 2
    k_ref[...] = _dot(hb, w_ref[:, :half])
    v_ref[...] = _dot(hb, w_ref[:, half:])


def _memory_kv(mem2d, g, w_bf16):
    t, d = mem2d.shape
    half = w_bf16.shape[1] // 2
    tm = min(PROJ_ROWS, t)
    return pl.pallas_call(
        _mem_kv_kernel,
        grid=(t // tm,),
        in_specs=[pl.BlockSpec((tm, d), lambda i: (i, 0)), pl.BlockSpec((1, d), lambda i: (0, 0)), _whole_vmem()],
        out_specs=[pl.BlockSpec((tm, half), lambda i: (i, 0))] * 2,
        out_shape=[jax.ShapeDtypeStruct((t, half), F32)] * 2,
        compiler_params=_params(),
        name="memory_kv",
    )(mem2d, g.reshape(1, d), w_bf16)


def _mem_head_attention(q, k, v):
    s = _dot_nt(q, k)
    p = jnp.exp(s - jnp.max(s, axis=1, keepdims=True))
    return _dot(p.astype(BF16), v) * (1.0 / jnp.sum(p, axis=1, keepdims=True))


def _mem_attn_prompt_kernel(q_ref, k_ref, v_ref, o_ref, *, heads, dim):
    for h in range(heads):
        sl = slice(h * dim, (h + 1) * dim)
        o_ref[0, :, sl] = _mem_head_attention(q_ref[0, :, sl], k_ref[0, :, sl], v_ref[0, :, sl])


def _memory_attention_prompt(q, mk, mv, heads, dim):
    b, s, w = q.shape
    tm = min(MEM_ROWS, s)
    m = mk.shape[1]
    return pl.pallas_call(
        functools.partial(_mem_attn_prompt_kernel, heads=heads, dim=dim),
        grid=(b, s // tm),
        in_specs=[pl.BlockSpec((1, tm, w), lambda bi, i: (bi, i, 0)),
                  pl.BlockSpec((1, m, w), lambda bi, i: (bi, 0, 0)),
                  pl.BlockSpec((1, m, w), lambda bi, i: (bi, 0, 0))],
        out_specs=pl.BlockSpec((1, tm, w), lambda bi, i: (bi, i, 0)),
        out_shape=jax.ShapeDtypeStruct((b, s, w), F32),
        compiler_params=_params(2),
        name="memory_attn_prompt",
    )(q, mk, mv)


def _mem_attn_sample_kernel(q_ref, k_ref, v_ref, o_ref, *, heads, dim, t, group):
    for bi in range(group):
        for h in range(heads):
            sl = slice(h * dim, (h + 1) * dim)
            o = _mem_head_attention(q_ref[:, sl].astype(BF16), k_ref[bi, :, sl], v_ref[bi, :, sl])
            o_ref[bi * t:(bi + 1) * t, sl] = o[bi * t:(bi + 1) * t]


def _memory_attention_sample(q, mk, mv, heads, dim, t):
    db, m, w = mk.shape
    group = min(SAMPLE_GROUP, db)
    mem_spec = pl.BlockSpec((group, m, w), lambda i: (i, 0, 0))
    return pl.pallas_call(
        functools.partial(_mem_attn_sample_kernel, heads=heads, dim=dim, t=t, group=group),
        grid=(db // group,),
        in_specs=[pl.BlockSpec((group * t, w), lambda i: (i, 0)), mem_spec, mem_spec],
        out_specs=pl.BlockSpec((group * t, w), lambda i: (i, 0)),
        out_shape=jax.ShapeDtypeStruct((db * t, w), F32),
        compiler_params=_params(),
        name="memory_attn_sample",
    )(q, mk, mv)


def _merge_ffn_kernel(x_ref, gt_ref, oa_ref, or_ref, om_ref, wo_ref, fg_ref, wg_ref, wu_ref, wd_ref, fin_ref,
                      y_ref, *, final_norm):
    d = x_ref.shape[1]
    u = (gt_ref[:, :d] * oa_ref[...] + gt_ref[:, d:2 * d] * or_ref[...] + gt_ref[:, 2 * d:] * om_ref[...])
    x1 = x_ref[...] + _dot(u.astype(BF16), wo_ref[...])
    hb = _rmsnorm(x1, fg_ref[...]).astype(BF16)
    a = _dot(hb, wg_ref[...])
    act = (a * _sigmoid(a)) * _dot(hb, wu_ref[...])
    x2 = x1 + _dot(act.astype(BF16), wd_ref[...])
    y_ref[...] = _rmsnorm(x2, fin_ref[...]) if final_norm else x2


def _merge_ffn(x2d, gates, o_a, o_r, o_m, w_out, ffn_g, w_gate, w_up, w_down, final_g, final_norm):
    t, d = x2d.shape
    tm = min(PROJ_ROWS, t)
    row = lambda w: pl.BlockSpec((tm, w), lambda i: (i, 0))
    vec = pl.BlockSpec((1, d), lambda i: (0, 0))
    return pl.pallas_call(
        functools.partial(_merge_ffn_kernel, final_norm=final_norm),
        grid=(t // tm,),
        in_specs=[row(d), row(gates.shape[1]), row(d), row(d), row(d), _whole_vmem(), vec,
                  _whole_vmem(), _whole_vmem(), _whole_vmem(), vec],
        out_specs=row(d),
        out_shape=jax.ShapeDtypeStruct((t, d), F32),
        compiler_params=_params(),
        name="merge_ffn",
    )(x2d, gates, o_a, o_r, o_m, w_out, ffn_g.reshape(1, d), w_gate, w_up, w_down, final_g.reshape(1, d))


def kernel(x_prompt, x_sample, cache_k, cache_v, cache_idx_k, state_ret, cache_mem_k, cache_mem_v, page_table,
           mem_prompt, attn_norm_g, w_in, ret_gn_g, w_out, ffn_norm_g, w_gate, w_up, w_down, mem_norm_g, w_mem_kv,
           final_norm_g):
    depth = w_in.shape[0]
    b, s, d = x_prompt.shape
    db, t, _ = x_sample.shape
    _, n_pool, page, kv_heads, head_dim = cache_k.shape
    idx_dim = cache_idx_k.shape[-1]
    _, _, ret_heads, ret_dk, ret_dv = state_ret.shape
    _, _, mem_tokens, mem_heads, mem_dim = cache_mem_k.shape
    q_heads = d // head_dim
    group = q_heads // kv_heads
    kvw = kv_heads * head_dim
    n_pages = page_table.shape[1]
    past = n_pages * page
    dims = dict(idx_dim=idx_dim, ret_dk=ret_dk, head_dim=head_dim, mem_dim=mem_dim)
    lay, _ = _proj_layout(d, kvw, idx_dim, ret_heads * ret_dk, ret_heads * ret_dv, mem_heads * mem_dim,
                          w_in.shape[2] - (d + 2 * kvw + IDX_HEADS * idx_dim + idx_dim + IDX_HEADS
                                           + 2 * ret_heads * ret_dk + 2 * ret_heads * ret_dv + mem_heads * mem_dim))
    o_ki = d + 2 * kvw + IDX_HEADS * idx_dim
    o_qr = o_ki + idx_dim + IDX_HEADS

    cos_p, sin_p = _rotary_tables(jnp.arange(s), ret_dk, min(PROJ_ROWS, b * s))
    cos_s, sin_s = _rotary_tables(past + jnp.arange(t), ret_dk, min(PROJ_ROWS, db * t))
    tri = (jnp.arange(TRI_W)[:, None] <= jnp.arange(TRI_W)[None, :]).astype(BF16)
    topk_p = min(TOPK_MAX, s // 4)
    topk_s = min(TOPK_MAX, (past + t) // 4)

    xp = x_prompt.reshape(b * s, d)
    xs = x_sample.reshape(db * t, d)
    outs = {k: [] for k in ("k_p", "v_p", "ik_p", "rs_p", "mk_p", "mv_p", "k_s", "v_s", "ik_s", "rs_s")}
    for l in range(depth):
        last = l == depth - 1
        wl = w_in[l]
        w_packed = jnp.concatenate(
            [wl[:, :o_ki], wl[:, o_ki:o_qr], jnp.zeros((d, LANES - (o_qr - o_ki)), wl.dtype), wl[:, o_qr:]],
            axis=1).astype(BF16)
        w_out_b, w_gate_b, w_up_b, w_down_b = (a[l].astype(BF16) for a in (w_out, w_gate, w_up, w_down))

        p = _project(xp, attn_norm_g[l], w_packed, cos_p, sin_p, lay, dims, BF16)
        qi = p["qi"].reshape(b * s, IDX_HEADS, idx_dim).transpose(1, 0, 2)
        qa = p["qa"].reshape(b * s, q_heads, head_dim).transpose(1, 0, 2)
        kit = p["ki"].astype(BF16).reshape(b, s, idx_dim).transpose(0, 2, 1)
        kt = p["ka"].astype(BF16).reshape(b, s, kvw).transpose(0, 2, 1)
        va = p["va"].astype(BF16).reshape(b, s, kv_heads, head_dim)
        vb = jnp.concatenate([va, jnp.ones((b, s, kv_heads, LANES - head_dim), BF16)], axis=-1)
        o_a = _sparse_attention_prompt(qi, p["wi"], qa, kit, kt, vb.reshape(b, s, kv_heads * LANES), tri, topk_p,
                                       kv_heads, head_dim)
        o_r, st_p = _retention_prompt(p["qr"].reshape(b, s, -1), p["kr"].reshape(b, s, -1),
                                      p["vr"].reshape(b, s, -1), p["gr"].reshape(b, s, -1), ret_gn_g[l],
                                      ret_heads, ret_dk, ret_dv)
        mk, mv = _memory_kv(mem_prompt.reshape(b * mem_tokens, d), mem_norm_g[l], w_mem_kv[l].astype(BF16))
        o_m = _memory_attention_prompt(p["qm"].reshape(b, s, -1), mk.astype(BF16).reshape(b, mem_tokens, -1),
                                       mv.astype(BF16).reshape(b, mem_tokens, -1), mem_heads, mem_dim)
        xp = _merge_ffn(xp, p["gt"], o_a, o_r.reshape(b * s, -1), o_m.reshape(b * s, -1), w_out_b, ffn_norm_g[l],
                        w_gate_b, w_up_b, w_down_b, final_norm_g, last)
        outs["k_p"].append(p["ka"].reshape(b, s, kv_heads, head_dim))
        outs["v_p"].append(p["va"].reshape(b, s, kv_heads, head_dim))
        outs["ik_p"].append(p["ki"].reshape(b, s, idx_dim))
        outs["rs_p"].append(st_p)
        outs["mk_p"].append(mk.reshape(b, mem_tokens, mem_heads, mem_dim))
        outs["mv_p"].append(mv.reshape(b, mem_tokens, mem_heads, mem_dim))

        q = _project(xs, attn_norm_g[l], w_packed, cos_s, sin_s, lay, dims, F32)
        qi_s = q["qi"].reshape(db, t, IDX_HEADS, idx_dim).transpose(0, 2, 1, 3).reshape(db, IDX_HEADS * t, idx_dim)
        qa_s = q["qa"].reshape(db, t, kv_heads, group, head_dim).transpose(0, 2, 3, 1, 4)
        qbd = (qa_s.reshape(db, kv_heads, group * t, 1, head_dim)
               * jnp.eye(kv_heads, dtype=F32)[None, :, None, :, None]).reshape(db, kv_heads * group * t, kvw)
        pad_rows = lambda a: jnp.pad(a.reshape(db, t, -1), ((0, 0), (0, LANES - t), (0, 0))).astype(BF16)
        o_a = _sparse_attention_sample(
            page_table, qi_s.astype(BF16), q["wi"].reshape(db, t, IDX_HEADS), qbd.astype(BF16),
            pad_rows(q["ki"]), pad_rows(q["ka"]), pad_rows(q["va"]), tri,
            jnp.swapaxes(cache_idx_k[l], 1, 2),
            cache_k[l].transpose(0, 2, 3, 1).reshape(n_pool, kvw, page),
            cache_v[l].transpose(0, 2, 3, 1).reshape(n_pool, kvw, page),
            topk_s, kv_heads, group, head_dim).reshape(db * t, d)
        o_r, st_s = _retention_sample(state_ret[l], q["qr"], q["kr"], q["vr"], q["gr"], ret_gn_g[l],
                                      ret_heads, ret_dk, ret_dv, t)
        o_m = _memory_attention_sample(q["qm"], cache_mem_k[l].reshape(db, mem_tokens, -1).astype(BF16),
                                       cache_mem_v[l].reshape(db, mem_tokens, -1).astype(BF16), mem_heads, mem_dim, t)
        xs = _merge_ffn(xs, q["gt"], o_a, o_r, o_m, w_out_b, ffn_norm_g[l], w_gate_b, w_up_b, w_down_b,
                        final_norm_g, last)
        outs["k_s"].append(q["ka"].reshape(db, t, kv_heads, head_dim))
        outs["v_s"].append(q["va"].reshape(db, t, kv_heads, head_dim))
        outs["ik_s"].append(q["ki"].reshape(db, t, idx_dim))
        outs["rs_s"].append(st_s)

    stk = lambda k: jnp.stack(outs[k])
    return (xp.reshape(b, s, d), xs.reshape(db, t, d), stk("k_p"), stk("v_p"), stk("ik_p"), stk("rs_p"),
            stk("mk_p"), stk("mv_p"), stk("k_s"), stk("v_s"), stk("ik_s"), stk("rs_s"))
```

```python
import functools
import math

import numpy as np
import jax
import jax.numpy as jnp
from jax import lax
from jax.experimental import pallas as pl
from jax.experimental.pallas import tpu as pltpu

F32 = jnp.float32
BF16 = jnp.bfloat16
I32 = jnp.int32

IDX_HEADS = 4
TOPK_MAX = 256
RET_CHUNK = 128
RET_DECAY_OFFSET = 5.0
ROPE_BASE = 10000.0
EPS = 1e-6

LANES = 128
VMEM_LIMIT_BYTES = 56 * 1024 * 1024

NEG = -1e30
INT_MIN = -(2 ** 31)
LOG2E = math.log2(math.e)
MIN_NORMAL = 2.0 ** -126

PROJ_ROWS = 256
ATTN_TQ = 128
ATTN_KB = 1024
SAMPLE_KB = 1024
COUNT_ELEMS = 65536
TRI_W = 256
PREFILTER_DEPTH = 12
PREFILTER_MIN_CHUNKS = 8
MEM_ROWS = 512
SAMPLE_GROUP = 8


def _params(n_axes=1):
    return pltpu.CompilerParams(dimension_semantics=("arbitrary",) * n_axes,
                                vmem_limit_bytes=VMEM_LIMIT_BYTES)


def _whole_vmem():
    return pl.BlockSpec(memory_space=pltpu.VMEM)


def _rmsnorm(x, g):
    return (x * lax.rsqrt(jnp.mean(x * x, axis=-1, keepdims=True) + EPS)) * g


def _sigmoid(x):
    return 1.0 / (1.0 + jnp.exp(-x))


def _dot(a, b):
    return jnp.dot(a, b, preferred_element_type=F32)


def _dot_nt(a, b):
    return lax.dot_general(a, b, (((1,), (1,)), ((), ())), preferred_element_type=F32)


def _proj_layout(d_model, kv_w, idx_dim, ret_qk_w, ret_v_w, mem_w, n_gate):
    segs = [("qa", d_model), ("ka", kv_w), ("va", kv_w), ("qi", IDX_HEADS * idx_dim), ("kw", LANES),
            ("qr", ret_qk_w), ("kr", ret_qk_w), ("vr", ret_v_w), ("gr", ret_v_w), ("qm", mem_w),
            ("gt", n_gate)]
    out, off = {}, 0
    for name, w in segs:
        out[name] = (off, off + w)
        off += w
    return out, off


def _proj_kernel(x_ref, g_ref, w_ref, cos_ref, sin_ref,
                 qa_ref, ka_ref, va_ref, qi_ref, ki_ref, wi_ref, qr_ref, kr_ref, vr_ref, gr_ref, qm_ref, gt_ref,
                 *, lay, idx_dim, ret_dk, qa_scale, wi_scale, kr_scale, qm_scale):
    hb = _rmsnorm(x_ref[...], g_ref[...]).astype(BF16)

    def seg(name):
        lo, hi = lay[name]
        return _dot(hb, w_ref[:, lo:hi])

    qa_ref[...] = (seg("qa") * qa_scale).astype(qa_ref.dtype)
    ka_ref[...] = seg("ka")
    va_ref[...] = seg("va")
    qi_ref[...] = seg("qi").astype(qi_ref.dtype)
    kw = seg("kw")
    ki_ref[...] = kw[:, :idx_dim]
    wi_ref[...] = kw[:, idx_dim:idx_dim + IDX_HEADS] * wi_scale

    cos2 = cos_ref[...]
    sin2 = sin_ref[...]

    def rotary(z, out_ref, scale):
        for h in range(z.shape[1] // ret_dk):
            zh = z[:, h * ret_dk:(h + 1) * ret_dk]
            r = zh * cos2 + pltpu.roll(zh, ret_dk // 2, 1) * sin2
            out_ref[:, h * ret_dk:(h + 1) * ret_dk] = r if scale is None else r * scale

    rotary(seg("qr"), qr_ref, None)
    rotary(seg("kr"), kr_ref, kr_scale)
    vr_ref[...] = seg("vr")
    gr_ref[...] = seg("gr")
    qm_ref[...] = (seg("qm") * qm_scale).astype(qm_ref.dtype)
    gt_ref[...] = _sigmoid(seg("gt"))


def _project(x2d, g, w_packed, cos2, sin2, lay, dims, q_dtype):
    t, d = x2d.shape
    tm = min(PROJ_ROWS, t)
    n_pos_blocks = cos2.shape[0] // tm
    widths = {k: hi - lo for k, (lo, hi) in lay.items()}
    out_defs = [("qa", widths["qa"], q_dtype), ("ka", widths["ka"], F32), ("va", widths["va"], F32),
                ("qi", widths["qi"], q_dtype), ("ki", dims["idx_dim"], F32), ("wi", IDX_HEADS, F32),
                ("qr", widths["qr"], F32), ("kr", widths["kr"], F32), ("vr", widths["vr"], F32),
                ("gr", widths["gr"], F32), ("qm", widths["qm"], q_dtype), ("gt", widths["gt"], F32)]
    row = lambda w: pl.BlockSpec((tm, w), lambda i: (i, 0))
    kern = functools.partial(
        _proj_kernel, lay=lay, idx_dim=dims["idx_dim"], ret_dk=dims["ret_dk"],
        qa_scale=dims["head_dim"] ** -0.5 * LOG2E, wi_scale=(IDX_HEADS ** -0.5) * (dims["idx_dim"] ** -0.5),
        kr_scale=dims["ret_dk"] ** -0.5, qm_scale=dims["mem_dim"] ** -0.5)
    outs = pl.pallas_call(
        kern,
        grid=(t // tm,),
        in_specs=[row(d), pl.BlockSpec((1, d), lambda i: (0, 0)), _whole_vmem(),
                  pl.BlockSpec((tm, LANES), lambda i: (i % n_pos_blocks, 0)),
                  pl.BlockSpec((tm, LANES), lambda i: (i % n_pos_blocks, 0))],
        out_specs=[row(w) for _, w, _ in out_defs],
        out_shape=[jax.ShapeDtypeStruct((t, w), dt) for _, w, dt in out_defs],
        compiler_params=_params(),
        name="proj",
    )(x2d, g.reshape(1, d), w_packed, cos2, sin2)
    return {name: o for (name, _, _), o in zip(out_defs, outs)}


def _rotary_tables(pos, ret_dk, rows):
    half = ret_dk // 2
    inv = ROPE_BASE ** (-jnp.arange(half, dtype=F32) / half)
    ang = pos.astype(F32)[:, None] * inv[None, :]
    cos, sin = jnp.cos(ang), jnp.sin(ang)
    cos2 = jnp.concatenate([cos, cos], axis=1)
    sin2 = jnp.concatenate([-sin, sin], axis=1)
    reps = max(1, rows // cos2.shape[0])
    return jnp.tile(cos2, (reps, 1)), jnp.tile(sin2, (reps, 1))


def _index_scores(s4, w):
    acc = w[:, 0:1] * jnp.maximum(s4[0], 0.0)
    for h in range(1, IDX_HEADS):
        acc = acc + w[:, h:h + 1] * jnp.maximum(s4[h], 0.0)
    return jnp.where(jnp.abs(acc) < MIN_NORMAL, 0.0, acc)


def _sortable_key(bits):
    return jnp.where(bits >= 0, bits, bits ^ 0x7FFFFFFF)


def _count_chunk(rows):
    return max(LANES, COUNT_ELEMS // rows)


def _tree_sum(parts):
    while len(parts) > 1:
        parts = [a + b for a, b in zip(parts[::2], parts[1::2])] + parts[len(parts) & ~1:]
    return parts[0]


def _count_ge(keys_ref, rows, n_chunks, cand):
    cw = _count_chunk(rows)
    cb = jnp.broadcast_to(cand, (rows, LANES))

    def body(c, a):
        blk = keys_ref[:, pl.ds(pl.multiple_of(c * cw, cw), cw)]
        return a + _tree_sum([jnp.where(blk[:, u * LANES:(u + 1) * LANES] >= cb, 1.0, 0.0)
                              for u in range(cw // LANES)])

    a = lax.fori_loop(0, n_chunks, body, jnp.zeros((rows, LANES), F32), unroll=isinstance(n_chunks, int))
    return jnp.sum(a, axis=1, keepdims=True)


def _kth_largest_key(keys_ref, rows, n_chunks, topk, bracket=None):
    def bit_step(b, u):
        cand_u = u | lax.shift_left(jnp.int32(1), 31 - b)
        return jnp.where(_count_ge(keys_ref, rows, n_chunks, cand_u ^ INT_MIN) >= topk, cand_u, u)

    if bracket is None:
        first, u0 = 0, jnp.zeros((rows, 1), I32)
    else:
        lo, hi = bracket
        first = jnp.min(lax.clz(lo ^ hi).astype(F32)).astype(I32)
        prefix = jnp.where(first == 0, 0, lax.shift_left(jnp.int32(-1), 32 - jnp.maximum(first, 1)))
        u0 = (hi ^ INT_MIN) & prefix
    return lax.fori_loop(first, 32, bit_step, u0) ^ INT_MIN


def _tie_quota(thr, n_greater, topk):
    return jnp.where(thr == INT_MIN, 0.0, topk - n_greater)


def _topk_threshold(keys_ref, rows, n_chunks, topk):
    thr = _kth_largest_key(keys_ref, rows, n_chunks, topk)
    return thr, _tie_quota(thr, _count_ge(keys_ref, rows, n_chunks, thr + 1), topk)


def _lane_top_insert(cand_ref, sc_ref, rows, width):
    depth = cand_ref.shape[0]

    def body(g, c):
        r = pl.ds(pl.multiple_of(g * 8, 8), 8)
        tops = [cand_ref[m, r, :] for m in range(depth)]
        for c0 in range(0, width, LANES):
            x = sc_ref[r, c0:c0 + LANES]
            for m in range(depth):
                tops[m], x = jnp.maximum(tops[m], x), jnp.minimum(tops[m], x)
        for m in range(depth):
            cand_ref[m, r, :] = tops[m]
        return c

    lax.fori_loop(0, rows // 8, body, 0)


def _selection_bias(keys, thr, need, run, tri):
    tw = tri.shape[0]
    rows, n_blocks = keys.shape[0], keys.shape[1] // tw
    eq = keys == thr
    gt = keys > thr
    eq_b = jnp.where(eq, 1.0, 0.0).astype(BF16)
    blocks = [eq_b[:, c * tw:(c + 1) * tw] for c in range(n_blocks)]
    if n_blocks > 1 and rows % 16 == 0:
        stacked = _dot(jnp.concatenate(blocks, axis=0), tri)
        counts = [stacked[c * rows:(c + 1) * rows] for c in range(n_blocks)]
    else:
        counts = [_dot(blk, tri) for blk in blocks]
    parts = []
    for cnt in counts:
        pre = cnt + run
        run = pre[:, tw - 1:]
        parts.append(pre)
    prefix = parts[0] if len(parts) == 1 else jnp.concatenate(parts, axis=1)
    bias = jnp.where(gt, 0.0, jnp.where(eq, jnp.where(prefix <= need, 0.0, NEG), NEG))
    return bias, run


def _lane_tile(x, width):
    reps = width // x.shape[1]
    return x if reps == 1 else jnp.concatenate([x] * reps, axis=1)


def _softmax_tile(s, pv, m_ref, l_ref, acc_ref, idx):
    m_prev = m_ref[idx]
    m_new = jnp.maximum(m_prev, jnp.max(s, axis=1, keepdims=True))
    alpha = jnp.exp2(m_prev - m_new)
    p = jnp.exp2(s - _lane_tile(m_new, s.shape[1]))
    l_ref[idx] = alpha * l_ref[idx] + jnp.sum(p, axis=1, keepdims=True)
    acc = acc_ref[idx]
    acc_ref[idx] = _lane_tile(alpha, acc.shape[1]) * acc + pv(p.astype(BF16))
    m_ref[idx] = m_new


def _sparse_prompt_kernel(qi_ref, wi_ref, qa_ref, kit_ref, kt_ref, v_ref, tri_ref, o_ref,
                          keys_ref, sc_ref, cand_ref, ckeys_ref, p_ref, m_ref, acc_ref,
                          *, tq, kb, topk, kv_heads, group, head_dim):
    i = pl.program_id(1)
    n_tiles = ((i + 1) * tq + kb - 1) // kb
    row_pos = i * tq + lax.broadcasted_iota(I32, (tq, kb), 0)
    col_iota = lax.broadcasted_iota(I32, (tq, kb), 1)
    w = wi_ref[...]
    qi = qi_ref[...].reshape(IDX_HEADS * tq, qi_ref.shape[2])
    cw = _count_chunk(tq)
    n_chunks = n_tiles * (kb // cw)
    depth = cand_ref.shape[0]
    cand_chunks = depth * LANES // cw
    prefilter = n_chunks >= PREFILTER_MIN_CHUNKS

    @pl.when(prefilter)
    def _():
        cand_ref[...] = jnp.full(cand_ref.shape, -jnp.inf, F32)

    def score_tile(j, c):
        k0 = pl.multiple_of(j * kb, kb)
        s4 = _dot(qi, kit_ref[0, :, pl.ds(k0, kb)]).reshape(IDX_HEADS, tq, kb)
        sc = _index_scores(s4, w)
        visible = k0 + col_iota <= row_pos
        keys_ref[:, pl.ds(k0, kb)] = jnp.where(visible, _sortable_key(lax.bitcast_convert_type(sc, I32)), INT_MIN)

        @pl.when(prefilter)
        def _():
            sc_ref[...] = jnp.where(visible, sc, -jnp.inf)
            _lane_top_insert(cand_ref, sc_ref, tq, kb)

        return c

    lax.fori_loop(0, n_tiles, score_tile, 0)

    def ranked_directly():
        return _topk_threshold(keys_ref, tq, n_chunks, topk)

    def ranked_from_lane_tops():
        for m in range(depth):
            ckeys_ref[:, m * LANES:(m + 1) * LANES] = _sortable_key(lax.bitcast_convert_type(cand_ref[m], I32))
        bracket = None
        if topk > LANES and topk <= 2 * LANES and depth >= 2:
            second = cand_ref[1]
            as_key = lambda x: _sortable_key(lax.bitcast_convert_type(x, I32))
            bracket = (as_key(jnp.min(second, axis=1, keepdims=True)), as_key(jnp.max(second, axis=1, keepdims=True)))
        thr = _kth_largest_key(ckeys_ref, tq, cand_chunks, topk, bracket)
        n_greater = _count_ge(keys_ref, tq, n_chunks, thr + 1)
        kept = _count_ge(ckeys_ref, tq, cand_chunks, thr + 1)
        missed = jnp.max(n_greater - kept)
        return lax.cond(missed == 0.0, lambda: (thr, _tie_quota(thr, n_greater, topk)), ranked_directly)

    thr, need = lax.cond(prefilter, ranked_from_lane_tops, ranked_directly)

    m_ref[...] = jnp.full(m_ref.shape, NEG, F32)
    acc_ref[...] = jnp.zeros(acc_ref.shape, F32)

    def attn_tile(j, run):
        k0 = pl.multiple_of(j * kb, kb)
        bias, run = _selection_bias(keys_ref[:, pl.ds(k0, kb)], thr, need, run, tri_ref[...])
        gr = group * tq
        for pair in range(kv_heads // 2):
            alphas = []
            for half in range(2):
                n = 2 * pair + half
                qn = qa_ref[n * group:(n + 1) * group].reshape(gr, head_dim)
                s = _dot(qn, kt_ref[0, n * head_dim:(n + 1) * head_dim, pl.ds(k0, kb)])
                s = (s.reshape(group, tq, kb) + bias[None]).reshape(gr, kb)
                m_prev = m_ref[n]
                m_new = jnp.maximum(m_prev, jnp.max(s, axis=1, keepdims=True))
                alphas.append(jnp.exp2(m_prev - m_new))
                p_ref[half * gr:(half + 1) * gr, :] = jnp.exp2(s - _lane_tile(m_new, kb)).astype(BF16)
                m_ref[n] = m_new
            pv = _dot(p_ref[...], v_ref[0, pl.ds(k0, kb), pair * 2 * LANES:(pair + 1) * 2 * LANES])
            for half in range(2):
                n = 2 * pair + half
                acc_ref[n] = alphas[half] * acc_ref[n] + pv[half * gr:(half + 1) * gr, half * LANES:(half + 1) * LANES]
        return run

    lax.fori_loop(0, n_tiles, attn_tile, jnp.zeros((tq, 1), F32))

    for n in range(kv_heads):
        acc = acc_ref[n]
        o = acc * pltpu.roll(1.0 / acc, LANES - head_dim, 1)
        for g in range(group):
            h = n * group + g
            o_ref[:, h * head_dim:(h + 1) * head_dim] = o[g * tq:(g + 1) * tq, :head_dim]


def _sparse_attention_prompt(qi, wi, qa, kit, kt, v, tri, topk, kv_heads, head_dim):
    q_heads = qa.shape[0]
    b, s, _ = v.shape
    tq, kb = min(ATTN_TQ, s), min(ATTN_KB, s)
    nq = s // tq
    group = q_heads // kv_heads
    kern = functools.partial(_sparse_prompt_kernel, tq=tq, kb=kb, topk=topk, kv_heads=kv_heads,
                             group=group, head_dim=head_dim)
    resident = lambda shape: pl.BlockSpec((1,) + shape, lambda bi, i: (bi, 0, 0), pipeline_mode=pl.Buffered(1))
    return pl.pallas_call(
        kern,
        grid=(b, nq),
        in_specs=[pl.BlockSpec((IDX_HEADS, tq, qi.shape[2]), lambda bi, i: (0, bi * nq + i, 0)),
                  pl.BlockSpec((tq, IDX_HEADS), lambda bi, i: (bi * nq + i, 0)),
                  pl.BlockSpec((q_heads, tq, head_dim), lambda bi, i: (0, bi * nq + i, 0)),
                  resident(kit.shape[1:]), resident(kt.shape[1:]), resident(v.shape[1:]), _whole_vmem()],
        out_specs=pl.BlockSpec((tq, q_heads * head_dim), lambda bi, i: (bi * nq + i, 0)),
        out_shape=jax.ShapeDtypeStruct((b * s, q_heads * head_dim), F32),
        scratch_shapes=[pltpu.VMEM((tq, s), I32),
                        pltpu.VMEM((tq, kb), F32),
                        pltpu.VMEM((PREFILTER_DEPTH, tq, LANES), F32),
                        pltpu.VMEM((tq, PREFILTER_DEPTH * LANES), I32),
                        pltpu.VMEM((2 * group * tq, kb), BF16),
                        pltpu.VMEM((kv_heads, group * tq, LANES), F32),
                        pltpu.VMEM((kv_heads, group * tq, LANES), F32)],
        compiler_params=_params(2),
        name="sparse_attn_prompt",
    )(qi, wi, qa, kit, kt, v, tri)


def _sparse_sample_kernel(pt_ref, qi_ref, wi_ref, qbd_ref, kin_ref, kan_ref, van_ref, tri_ref,
                          cidx_hbm, ck_hbm, cv_hbm, o_ref,
                          ibuf, kbuf, vbuf, sems, keys_ref, m_ref, l_ref, acc_ref,
                          *, t, kb, topk, n_pages, page, kv_heads, group, head_dim):
    b = pl.program_id(0)
    nb = pl.num_programs(0)
    past = n_pages * page
    rows = kv_heads * group * t
    cw = _count_chunk(t)

    def page_copies(bb, slot, p):
        pg = pt_ref[bb, p]
        dst = pl.ds(pl.multiple_of(p * page, page), page)
        return (pltpu.make_async_copy(cidx_hbm.at[pg], ibuf.at[slot, :, dst], sems.at[slot, 0]),
                pltpu.make_async_copy(ck_hbm.at[pg], kbuf.at[slot, :, dst], sems.at[slot, 1]),
                pltpu.make_async_copy(cv_hbm.at[pg], vbuf.at[slot, :, dst], sems.at[slot, 2]))

    def start_fetch(bb, slot):
        def body(p, c):
            for cp in page_copies(bb, slot, p):
                cp.start()
            return c
        lax.fori_loop(0, n_pages, body, 0)

    def wait_fetch(bb, slot):
        def body(p, c):
            for cp in page_copies(bb, slot, p):
                cp.wait()
            return c
        lax.fori_loop(0, n_pages, body, 0)

    slot = b % 2

    @pl.when(b == 0)
    def _():
        start_fetch(b, slot)

    @pl.when(b + 1 < nb)
    def _():
        start_fetch(b + 1, 1 - slot)

    wait_fetch(b, slot)

    w = wi_ref[0]
    qi = qi_ref[0]
    n_tiles = past // kb

    def score_tile(j, c):
        k0 = pl.multiple_of(j * kb, kb)
        s4 = _dot(qi, ibuf[slot, :, pl.ds(k0, kb)].astype(BF16)).reshape(IDX_HEADS, t, kb)
        keys_ref[:, pl.ds(k0, kb)] = _sortable_key(lax.bitcast_convert_type(_index_scores(s4, w), I32))
        return c

    lax.fori_loop(0, n_tiles, score_tile, 0)
    s4 = _dot_nt(qi, kin_ref[0]).reshape(IDX_HEADS, t, LANES)
    key_new = _sortable_key(lax.bitcast_convert_type(_index_scores(s4, w), I32))
    new_col = lax.broadcasted_iota(I32, (t, LANES), 1)
    new_row = lax.broadcasted_iota(I32, (t, LANES), 0)
    keys_ref[:, past:past + LANES] = jnp.where(new_col <= new_row, key_new, INT_MIN)
    keys_ref[:, past + LANES:past + cw] = jnp.full((t, cw - LANES), INT_MIN, I32)

    thr, need = _topk_threshold(keys_ref, t, past // cw + 1, topk)

    m_ref[...] = jnp.full(m_ref.shape, NEG, F32)
    l_ref[...] = jnp.zeros(l_ref.shape, F32)
    acc_ref[...] = jnp.zeros(acc_ref.shape, F32)
    qbd = qbd_ref[0]

    def tile_bias(bias, width):
        return jnp.broadcast_to(bias[None], (rows // t, t, width)).reshape(rows, width)

    def attn_tile(j, run):
        k0 = pl.multiple_of(j * kb, kb)
        bias, run = _selection_bias(keys_ref[:, pl.ds(k0, kb)], thr, need, run, tri_ref[...])
        s = _dot(qbd, kbuf[slot, :, pl.ds(k0, kb)].astype(BF16)) + tile_bias(bias, kb)
        vt = vbuf[slot, :, pl.ds(k0, kb)].astype(BF16)
        _softmax_tile(s, lambda p: _dot_nt(p, vt), m_ref, l_ref, acc_ref, 0)
        return run

    run = lax.fori_loop(0, n_tiles, attn_tile, jnp.zeros((t, 1), F32))
    bias, _ = _selection_bias(keys_ref[:, past:past + LANES], thr, need, run, tri_ref[:LANES, :LANES])
    s = _dot_nt(qbd, kan_ref[0]) + tile_bias(bias, LANES)
    van = van_ref[0]
    _softmax_tile(s, lambda p: _dot(p, van), m_ref, l_ref, acc_ref, 0)

    o = acc_ref[0] * (1.0 / _lane_tile(l_ref[0], acc_ref.shape[2]))
    for n in range(kv_heads):
        for g in range(group):
            h = n * group + g
            r0 = (n * group + g) * t
            o_ref[0, :, h * head_dim:(h + 1) * head_dim] = o[r0:r0 + t, n * head_dim:(n + 1) * head_dim]


def _sparse_attention_sample(page_table, qi, wi, qbd, ki_new, ka_new, va_new, tri, cache_idx_t, cache_k_t, cache_v_t,
                             topk, kv_heads, group, head_dim):
    db, t, _ = wi.shape
    n_pages = page_table.shape[1]
    page = cache_idx_t.shape[2]
    past = n_pages * page
    kb = min(SAMPLE_KB, past)
    rows = kv_heads * group * t
    kvw = kv_heads * head_dim
    cw = _count_chunk(t)
    kern = functools.partial(_sparse_sample_kernel, t=t, kb=kb, topk=topk, n_pages=n_pages, page=page,
                             kv_heads=kv_heads, group=group, head_dim=head_dim)
    per_b = lambda shape: pl.BlockSpec((1,) + shape, lambda b, pt: (b, 0, 0))
    grid_spec = pltpu.PrefetchScalarGridSpec(
        num_scalar_prefetch=1,
        grid=(db,),
        in_specs=[per_b(qi.shape[1:]), per_b(wi.shape[1:]), per_b(qbd.shape[1:]), per_b(ki_new.shape[1:]),
                  per_b(ka_new.shape[1:]), per_b(va_new.shape[1:]),
                  pl.BlockSpec(tri.shape, lambda b, pt: (0, 0)),
                  pl.BlockSpec(memory_space=pl.ANY), pl.BlockSpec(memory_space=pl.ANY),
                  pl.BlockSpec(memory_space=pl.ANY)],
        out_specs=per_b((t, kv_heads * group * head_dim)),
        scratch_shapes=[pltpu.VMEM((2, cache_idx_t.shape[1], past), F32),
                        pltpu.VMEM((2, kvw, past), F32),
                        pltpu.VMEM((2, kvw, past), F32),
                        pltpu.SemaphoreType.DMA((2, 3)),
                        pltpu.VMEM((t, past + cw), I32),
                        pltpu.VMEM((1, rows, LANES), F32),
                        pltpu.VMEM((1, rows, LANES), F32),
                        pltpu.VMEM((1, rows, kvw), F32)])
    return pl.pallas_call(
        kern,
        grid_spec=grid_spec,
        out_shape=jax.ShapeDtypeStruct((db, t, kv_heads * group * head_dim), F32),
        compiler_params=_params(),
        name="sparse_attn_sample",
    )(page_table, qi, wi, qbd, ki_new, ka_new, va_new, tri, cache_idx_t, cache_k_t, cache_v_t)


def _log_gamma(heads):
    return np.log(1.0 - np.exp2(-RET_DECAY_OFFSET - np.arange(heads, dtype=np.float64)))


def _retention_tables(heads, chunk, dk, group):
    lg = _log_gamma(heads)
    i = np.arange(chunk, dtype=np.float64)
    diff = i[:, None] - i[None, :]
    decay = np.where(diff >= 0, np.exp(lg[:, None, None] * np.maximum(diff, 0.0)), 0.0)
    decay_bd = np.zeros((heads, group * chunk, group * chunk))
    for gidx in range(group):
        decay_bd[:, gidx * chunk:(gidx + 1) * chunk, gidx * chunk:(gidx + 1) * chunk] = decay
    q_scale = np.tile(np.exp(lg[:, None] * (i + 1.0)), (1, group))
    k_scale = np.tile(np.exp(lg[:, None] * (chunk - 1.0 - i)), (1, group))
    state_scale = np.exp(lg * chunk)
    rows_q = np.broadcast_to(q_scale[:, :, None], (heads, group * chunk, dk))
    rows_k = np.broadcast_to(k_scale[:, :, None], (heads, group * chunk, dk))
    cols_k = np.broadcast_to(k_scale[:, None, :], (heads, dk, group * chunk))
    as32 = lambda a: jnp.asarray(np.ascontiguousarray(a), F32)
    return as32(decay_bd), as32(rows_q), as32(rows_k), as32(cols_k), [float(v) for v in state_scale]


def _groupnorm_gate(o, gn_g, g_r):
    mu = jnp.mean(o, axis=-1, keepdims=True)
    d = o - mu
    var = jnp.mean(d * d, axis=-1, keepdims=True)
    return (g_r * _sigmoid(g_r)) * ((d * lax.rsqrt(var + EPS)) * gn_g)


def _retention_prompt_kernel(q_ref, k_ref, v_ref, g_ref, gn_ref, dec_ref, qs_ref, ks_ref, o_ref, st_ref, state,
                             *, heads, dk, dv, state_scale):
    c = pl.program_id(1)

    @pl.when(c == 0)
    def _():
        state[...] = jnp.zeros(state.shape, F32)

    for h in range(heads):
        q = q_ref[0, :, h * dk:(h + 1) * dk]
        k = k_ref[0, :, h * dk:(h + 1) * dk]
        v = v_ref[0, :, h * dv:(h + 1) * dv].astype(BF16)
        s_old = state[h]
        inner = _dot_nt(q.astype(BF16), k.astype(BF16)) * dec_ref[h]
        o = _dot(inner.astype(BF16), v) + _dot((q * qs_ref[h]).astype(BF16), s_old.astype(BF16))
        state[h] = state_scale[h] * s_old + _dot((k * ks_ref[h]).T.astype(BF16), v)
        o_ref[0, :, h * dv:(h + 1) * dv] = _groupnorm_gate(o, gn_ref[:, h * dv:(h + 1) * dv],
                                                           g_ref[0, :, h * dv:(h + 1) * dv])

    @pl.when(c == pl.num_programs(1) - 1)
    def _():
        st_ref[0] = state[...]


def _retention_prompt(q, k, v, g, gn_g, heads, dk, dv):
    b, s, _ = q.shape
    chunk = min(RET_CHUNK, s)
    dec, qs, ks, _, state_scale = _retention_tables(heads, chunk, dk, 1)
    blk = lambda w: pl.BlockSpec((1, chunk, w), lambda bi, ci: (bi, ci, 0))
    const = lambda a: pl.BlockSpec(a.shape, lambda bi, ci: (0,) * a.ndim)
    gn2 = gn_g.reshape(1, heads * dv)
    kern = functools.partial(_retention_prompt_kernel, heads=heads, dk=dk, dv=dv, state_scale=state_scale)
    return pl.pallas_call(
        kern,
        grid=(b, s // chunk),
        in_specs=[blk(heads * dk), blk(heads * dk), blk(heads * dv), blk(heads * dv),
                  const(gn2), const(dec), const(qs), const(ks)],
        out_specs=[blk(heads * dv), pl.BlockSpec((1, heads, dk, dv), lambda bi, ci: (bi, 0, 0, 0))],
        out_shape=[jax.ShapeDtypeStruct((b, s, heads * dv), F32),
                   jax.ShapeDtypeStruct((b, heads, dk, dv), F32)],
        scratch_shapes=[pltpu.VMEM((heads, dk, dv), F32)],
        compiler_params=_params(2),
        name="retention_prompt",
    )(q, k, v, g, gn2, dec, qs, ks)


def _retention_sample_kernel(q_ref, k_ref, kt_ref, v_ref, g_ref, gn_ref, dec_ref, qs_ref, kst_ref, st_ref,
                             o_ref, sto_ref, *, heads, dk, dv, t, group, state_scale):
    rows = group * t
    col_batch = lax.broadcasted_iota(I32, (dk, rows), 1) // t
    for h in range(heads):
        q = q_ref[:, h * dk:(h + 1) * dk]
        k = k_ref[:, h * dk:(h + 1) * dk]
        v = v_ref[:, h * dv:(h + 1) * dv].astype(BF16)
        inner = _dot_nt(q.astype(BF16), k.astype(BF16)) * dec_ref[h]
        o = _dot(inner.astype(BF16), v)
        qd = (q * qs_ref[h]).astype(BF16)
        kdt = kt_ref[0, h] * kst_ref[h]
        cross = []
        for bi in range(group):
            s_old = st_ref[bi, h]
            cross.append(_dot(qd, s_old.astype(BF16))[bi * t:(bi + 1) * t])
            kd_b = jnp.where(col_batch == bi, kdt, 0.0).astype(BF16)
            sto_ref[bi, h] = state_scale[h] * s_old + _dot(kd_b, v)
        o = o + jnp.concatenate(cross, axis=0)
        o_ref[:, h * dv:(h + 1) * dv] = _groupnorm_gate(o, gn_ref[:, h * dv:(h + 1) * dv],
                                                        g_ref[:, h * dv:(h + 1) * dv])


def _retention_sample(state, q, k, v, g, gn_g, heads, dk, dv, t):
    db = state.shape[0]
    group = min(SAMPLE_GROUP, db)
    rows = group * t
    dec, qs, _, kst, state_scale = _retention_tables(heads, t, dk, group)
    kt = k.reshape(db // group, rows, heads, dk).transpose(0, 2, 3, 1)
    gn2 = gn_g.reshape(1, heads * dv)
    blk = lambda w: pl.BlockSpec((rows, w), lambda i: (i, 0))
    const = lambda a: pl.BlockSpec(a.shape, lambda i: (0,) * a.ndim)
    st_spec = pl.BlockSpec((group, heads, dk, dv), lambda i: (i, 0, 0, 0))
    kern = functools.partial(_retention_sample_kernel, heads=heads, dk=dk, dv=dv, t=t, group=group,
                             state_scale=state_scale)
    return pl.pallas_call(
        kern,
        grid=(db // group,),
        in_specs=[blk(heads * dk), blk(heads * dk),
                  pl.BlockSpec((1, heads, dk, rows), lambda i: (i, 0, 0, 0)),
                  blk(heads * dv), blk(heads * dv), const(gn2), const(dec), const(qs), const(kst), st_spec],
        out_specs=[blk(heads * dv), st_spec],
        out_shape=[jax.ShapeDtypeStruct((db * t, heads * dv), F32),
                   jax.ShapeDtypeStruct(state.shape, F32)],
        compiler_params=_params(),
        name="retention_sample",
    )(q, k, kt, v, g, gn2, dec, qs, kst, state)


def _mem_kv_kernel(x_ref, g_ref, w_ref, k_ref, v_ref):
    hb = _rmsnorm(x_ref[...], g_ref[...]).astype(BF16)
    half = w_ref.shape[1] // 2
    k_ref[...] = _dot(hb, w_ref[:, :half])
    v_ref[...] = _dot(hb, w_ref[:, half:])


def _memory_kv(mem2d, g, w_bf16):
    t, d = mem2d.shape
    half = w_bf16.shape[1] // 2
    tm = min(PROJ_ROWS, t)
    return pl.pallas_call(
        _mem_kv_kernel,
        grid=(t // tm,),
        in_specs=[pl.BlockSpec((tm, d), lambda i: (i, 0)), pl.BlockSpec((1, d), lambda i: (0, 0)), _whole_vmem()],
        out_specs=[pl.BlockSpec((tm, half), lambda i: (i, 0))] * 2,
        out_shape=[jax.ShapeDtypeStruct((t, half), F32)] * 2,
        compiler_params=_params(),
        name="memory_kv",
    )(mem2d, g.reshape(1, d), w_bf16)


def _mem_head_attention(q, k, v):
    s = _dot_nt(q, k)
    p = jnp.exp(s - jnp.max(s, axis=1, keepdims=True))
    return _dot(p.astype(BF16), v) * (1.0 / jnp.sum(p, axis=1, keepdims=True))


def _mem_attn_prompt_kernel(q_ref, k_ref, v_ref, o_ref, *, heads, dim):
    for h in range(heads):
        sl = slice(h * dim, (h + 1) * dim)
        o_ref[0, :, sl] = _mem_head_attention(q_ref[0, :, sl], k_ref[0, :, sl], v_ref[0, :, sl])


def _memory_attention_prompt(q, mk, mv, heads, dim):
    b, s, w = q.shape
    tm = min(MEM_ROWS, s)
    m = mk.shape[1]
    return pl.pallas_call(
        functools.partial(_mem_attn_prompt_kernel, heads=heads, dim=dim),
        grid=(b, s // tm),
        in_specs=[pl.BlockSpec((1, tm, w), lambda bi, i: (bi, i, 0)),
                  pl.BlockSpec((1, m, w), lambda bi, i: (bi, 0, 0)),
                  pl.BlockSpec((1, m, w), lambda bi, i: (bi, 0, 0))],
        out_specs=pl.BlockSpec((1, tm, w), lambda bi, i: (bi, i, 0)),
        out_shape=jax.ShapeDtypeStruct((b, s, w), F32),
        compiler_params=_params(2),
        name="memory_attn_prompt",
    )(q, mk, mv)


def _mem_attn_sample_kernel(q_ref, k_ref, v_ref, o_ref, *, heads, dim, t, group):
    for bi in range(group):
        for h in range(heads):
            sl = slice(h * dim, (h + 1) * dim)
            o = _mem_head_attention(q_ref[:, sl].astype(BF16), k_ref[bi, :, sl], v_ref[bi, :, sl])
            o_ref[bi * t:(bi + 1) * t, sl] = o[bi * t:(bi + 1) * t]


def _memory_attention_sample(q, mk, mv, heads, dim, t):
    db, m, w = mk.shape
    group = min(SAMPLE_GROUP, db)
    mem_spec = pl.BlockSpec((group, m, w), lambda i: (i, 0, 0))
    return pl.pallas_call(
        functools.partial(_mem_attn_sample_kernel, heads=heads, dim=dim, t=t, group=group),
        grid=(db // group,),
        in_specs=[pl.BlockSpec((group * t, w), lambda i: (i, 0)), mem_spec, mem_spec],
        out_specs=pl.BlockSpec((group * t, w), lambda i: (i, 0)),
        out_shape=jax.ShapeDtypeStruct((db * t, w), F32),
        compiler_params=_params(),
        name="memory_attn_sample",
    )(q, mk, mv)


def _merge_ffn_kernel(x_ref, gt_ref, oa_ref, or_ref, om_ref, wo_ref, fg_ref, wg_ref, wu_ref, wd_ref, fin_ref,
                      y_ref, *, final_norm):
    d = x_ref.shape[1]
    u = (gt_ref[:, :d] * oa_ref[...] + gt_ref[:, d:2 * d] * or_ref[...] + gt_ref[:, 2 * d:] * om_ref[...])
    x1 = x_ref[...] + _dot(u.astype(BF16), wo_ref[...])
    hb = _rmsnorm(x1, fg_ref[...]).astype(BF16)
    a = _dot(hb, wg_ref[...])
    act = (a * _sigmoid(a)) * _dot(hb, wu_ref[...])
    x2 = x1 + _dot(act.astype(BF16), wd_ref[...])
    y_ref[...] = _rmsnorm(x2, fin_ref[...]) if final_norm else x2


def _merge_ffn(x2d, gates, o_a, o_r, o_m, w_out, ffn_g, w_gate, w_up, w_down, final_g, final_norm):
    t, d = x2d.shape
    tm = min(PROJ_ROWS, t)
    row = lambda w: pl.BlockSpec((tm, w), lambda i: (i, 0))
    vec = pl.BlockSpec((1, d), lambda i: (0, 0))
    return pl.pallas_call(
        functools.partial(_merge_ffn_kernel, final_norm=final_norm),
        grid=(t // tm,),
        in_specs=[row(d), row(gates.shape[1]), row(d), row(d), row(d), _whole_vmem(), vec,
                  _whole_vmem(), _whole_vmem(), _whole_vmem(), vec],
        out_specs=row(d),
        out_shape=jax.ShapeDtypeStruct((t, d), F32),
        compiler_params=_params(),
        name="merge_ffn",
    )(x2d, gates, o_a, o_r, o_m, w_out, ffn_g.reshape(1, d), w_gate, w_up, w_down, final_g.reshape(1, d))


def kernel(x_prompt, x_sample, cache_k, cache_v, cache_idx_k, state_ret, cache_mem_k, cache_mem_v, page_table,
           mem_prompt, attn_norm_g, w_in, ret_gn_g, w_out, ffn_norm_g, w_gate, w_up, w_down, mem_norm_g, w_mem_kv,
           final_norm_g):
    depth = w_in.shape[0]
    b, s, d = x_prompt.shape
    db, t, _ = x_sample.shape
    _, n_pool, page, kv_heads, head_dim = cache_k.shape
    idx_dim = cache_idx_k.shape[-1]
    _, _, ret_heads, ret_dk, ret_dv = state_ret.shape
    _, _, mem_tokens, mem_heads, mem_dim = cache_mem_k.shape
    q_heads = d // head_dim
    group = q_heads // kv_heads
    kvw = kv_heads * head_dim
    n_pages = page_table.shape[1]
    past = n_pages * page
    dims = dict(idx_dim=idx_dim, ret_dk=ret_dk, head_dim=head_dim, mem_dim=mem_dim)
    lay, _ = _proj_layout(d, kvw, idx_dim, ret_heads * ret_dk, ret_heads * ret_dv, mem_heads * mem_dim,
                          w_in.shape[2] - (d + 2 * kvw + IDX_HEADS * idx_dim + idx_dim + IDX_HEADS
                                           + 2 * ret_heads * ret_dk + 2 * ret_heads * ret_dv + mem_heads * mem_dim))
    o_ki = d + 2 * kvw + IDX_HEADS * idx_dim
    o_qr = o_ki + idx_dim + IDX_HEADS

    cos_p, sin_p = _rotary_tables(jnp.arange(s), ret_dk, min(PROJ_ROWS, b * s))
    cos_s, sin_s = _rotary_tables(past + jnp.arange(t), ret_dk, min(PROJ_ROWS, db * t))
    tri = (jnp.arange(TRI_W)[:, None] <= jnp.arange(TRI_W)[None, :]).astype(BF16)
    topk_p = min(TOPK_MAX, s // 4)
    topk_s = min(TOPK_MAX, (past + t) // 4)

    xp = x_prompt.reshape(b * s, d)
    xs = x_sample.reshape(db * t, d)
    outs = {k: [] for k in ("k_p", "v_p", "ik_p", "rs_p", "mk_p", "mv_p", "k_s", "v_s", "ik_s", "rs_s")}
    for l in range(depth):
        last = l == depth - 1
        wl = w_in[l]
        w_packed = jnp.concatenate(
            [wl[:, :o_ki], wl[:, o_ki:o_qr], jnp.zeros((d, LANES - (o_qr - o_ki)), wl.dtype), wl[:, o_qr:]],
            axis=1).astype(BF16)
        w_out_b, w_gate_b, w_up_b, w_down_b = (a[l].astype(BF16) for a in (w_out, w_gate, w_up, w_down))

        p = _project(xp, attn_norm_g[l], w_packed, cos_p, sin_p, lay, dims, BF16)
        qi = p["qi"].reshape(b * s, IDX_HEADS, idx_dim).transpose(1, 0, 2)
        qa = p["qa"].reshape(b * s, q_heads, head_dim).transpose(1, 0, 2)
        kit = p["ki"].astype(BF16).reshape(b, s, idx_dim).transpose(0, 2, 1)
        kt = p["ka"].astype(BF16).reshape(b, s, kvw).transpose(0, 2, 1)
        va = p["va"].astype(BF16).reshape(b, s, kv_heads, head_dim)
        vb = jnp.concatenate([va, jnp.ones((b, s, kv_heads, LANES - head_dim), BF16)], axis=-1)
        o_a = _sparse_attention_prompt(qi, p["wi"], qa, kit, kt, vb.reshape(b, s, kv_heads * LANES), tri, topk_p,
                                       kv_heads, head_dim)
        o_r, st_p = _retention_prompt(p["qr"].reshape(b, s, -1), p["kr"].reshape(b, s, -1),
                                      p["vr"].reshape(b, s, -1), p["gr"].reshape(b, s, -1), ret_gn_g[l],
                                      ret_heads, ret_dk, ret_dv)
        mk, mv = _memory_kv(mem_prompt.reshape(b * mem_tokens, d), mem_norm_g[l], w_mem_kv[l].astype(BF16))
        o_m = _memory_attention_prompt(p["qm"].reshape(b, s, -1), mk.astype(BF16).reshape(b, mem_tokens, -1),
                                       mv.astype(BF16).reshape(b, mem_tokens, -1), mem_heads, mem_dim)
        xp = _merge_ffn(xp, p["gt"], o_a, o_r.reshape(b * s, -1), o_m.reshape(b * s, -1), w_out_b, ffn_norm_g[l],
                        w_gate_b, w_up_b, w_down_b, final_norm_g, last)
        outs["k_p"].append(p["ka"].reshape(b, s, kv_heads, head_dim))
        outs["v_p"].append(p["va"].reshape(b, s, kv_heads, head_dim))
        outs["ik_p"].append(p["ki"].reshape(b, s, idx_dim))
        outs["rs_p"].append(st_p)
        outs["mk_p"].append(mk.reshape(b, mem_tokens, mem_heads, mem_dim))
        outs["mv_p"].append(mv.reshape(b, mem_tokens, mem_heads, mem_dim))

        q = _project(xs, attn_norm_g[l], w_packed, cos_s, sin_s, lay, dims, F32)
        qi_s = q["qi"].reshape(db, t, IDX_HEADS, idx_dim).transpose(0, 2, 1, 3).reshape(db, IDX_HEADS * t, idx_dim)
        qa_s = q["qa"].reshape(db, t, kv_heads, group, head_dim).transpose(0, 2, 3, 1, 4)
        qbd = (qa_s.reshape(db, kv_heads, group * t, 1, head_dim)
               * jnp.eye(kv_heads, dtype=F32)[None, :, None, :, None]).reshape(db, kv_heads * group * t, kvw)
        pad_rows = lambda a: jnp.pad(a.reshape(db, t, -1), ((0, 0), (0, LANES - t), (0, 0))).astype(BF16)
        o_a = _sparse_attention_sample(
            page_table, qi_s.astype(BF16), q["wi"].reshape(db, t, IDX_HEADS), qbd.astype(BF16),
            pad_rows(q["ki"]), pad_rows(q["ka"]), pad_rows(q["va"]), tri,
            jnp.swapaxes(cache_idx_k[l], 1, 2),
            cache_k[l].transpose(0, 2, 3, 1).reshape(n_pool, kvw, page),
            cache_v[l].transpose(0, 2, 3, 1).reshape(n_pool, kvw, page),
            topk_s, kv_heads, group, head_dim).reshape(db * t, d)
        o_r, st_s = _retention_sample(state_ret[l], q["qr"], q["kr"], q["vr"], q["gr"], ret_gn_g[l],
                                      ret_heads, ret_dk, ret_dv, t)
        o_m = _memory_attention_sample(q["qm"], cache_mem_k[l].reshape(db, mem_tokens, -1).astype(BF16),
                                       cache_mem_v[l].reshape(db, mem_tokens, -1).astype(BF16), mem_heads, mem_dim, t)
        xs = _merge_ffn(xs, q["gt"], o_a, o_r, o_m, w_out_b, ffn_norm_g[l], w_gate_b, w_up_b, w_down_b,
                        final_norm_g, last)
        outs["k_s"].append(q["ka"].reshape(db, t, kv_heads, head_dim))
        outs["v_s"].append(q["va"].reshape(db, t, kv_heads, head_dim))
        outs["ik_s"].append(q["ki"].reshape(db, t, idx_dim))
        outs["rs_s"].append(st_s)

    stk = lambda k: jnp.stack(outs[k])
    return (xp.reshape(b, s, d), xs.reshape(db, t, d), stk("k_p"), stk("v_p"), stk("ik_p"), stk("rs_p"),
            stk("mk_p"), stk("mv_p"), stk("k_s"), stk("v_s"), stk("ik_s"), stk("rs_s"))
```

```python
import functools
import math

import numpy as np
import jax
import jax.numpy as jnp
from jax import lax
from jax.experimental import pallas as pl
from jax.experimental.pallas import tpu as pltpu

F32 = jnp.float32
BF16 = jnp.bfloat16
I32 = jnp.int32

IDX_HEADS = 4
TOPK_MAX = 256
RET_CHUNK = 128
RET_DECAY_OFFSET = 5.0
ROPE_BASE = 10000.0
EPS = 1e-6

LANES = 128
VMEM_LIMIT_BYTES = 56 * 1024 * 1024

NEG = -1e30
INT_MIN = -(2 ** 31)
LOG2E = math.log2(math.e)
MIN_NORMAL = 2.0 ** -126

PROJ_ROWS = 256
ATTN_TQ = 128
ATTN_KB = 1024
SAMPLE_KB = 1024
COUNT_ELEMS = 65536
TRI_W = 256
PREFILTER_DEPTH = 12
PREFILTER_MIN_CHUNKS = 8
MEM_ROWS = 512
SAMPLE_GROUP = 8


def _params(n_axes=1):
    return pltpu.CompilerParams(dimension_semantics=("arbitrary",) * n_axes,
                                vmem_limit_bytes=VMEM_LIMIT_BYTES)


def _whole_vmem():
    return pl.BlockSpec(memory_space=pltpu.VMEM)


def _rmsnorm(x, g):
    return (x * lax.rsqrt(jnp.mean(x * x, axis=-1, keepdims=True) + EPS)) * g


def _sigmoid(x):
    return 1.0 / (1.0 + jnp.exp(-x))


def _dot(a, b):
    return jnp.dot(a, b, preferred_element_type=F32)


def _dot_nt(a, b):
    return lax.dot_general(a, b, (((1,), (1,)), ((), ())), preferred_element_type=F32)


def _proj_layout(d_model, kv_w, idx_dim, ret_qk_w, ret_v_w, mem_w, n_gate):
    segs = [("qa", d_model), ("ka", kv_w), ("va", kv_w), ("qi", IDX_HEADS * idx_dim), ("kw", LANES),
            ("qr", ret_qk_w), ("kr", ret_qk_w), ("vr", ret_v_w), ("gr", ret_v_w), ("qm", mem_w),
            ("gt", n_gate)]
    out, off = {}, 0
    for name, w in segs:
        out[name] = (off, off + w)
        off += w
    return out, off


def _proj_kernel(x_ref, g_ref, w_ref, cos_ref, sin_ref,
                 qa_ref, ka_ref, va_ref, qi_ref, ki_ref, wi_ref, qr_ref, kr_ref, vr_ref, gr_ref, qm_ref, gt_ref,
                 *, lay, idx_dim, ret_dk, qa_scale, wi_scale, kr_scale, qm_scale):
    hb = _rmsnorm(x_ref[...], g_ref[...]).astype(BF16)

    def seg(name):
        lo, hi = lay[name]
        return _dot(hb, w_ref[:, lo:hi])

    qa_ref[...] = (seg("qa") * qa_scale).astype(qa_ref.dtype)
    ka_ref[...] = seg("ka")
    va_ref[...] = seg("va")
    qi_ref[...] = seg("qi").astype(qi_ref.dtype)
    kw = seg("kw")
    ki_ref[...] = kw[:, :idx_dim]
    wi_ref[...] = kw[:, idx_dim:idx_dim + IDX_HEADS] * wi_scale

    cos2 = cos_ref[...]
    sin2 = sin_ref[...]

    def rotary(z, out_ref, scale):
        for h in range(z.shape[1] // ret_dk):
            zh = z[:, h * ret_dk:(h + 1) * ret_dk]
            r = zh * cos2 + pltpu.roll(zh, ret_dk // 2, 1) * sin2
            out_ref[:, h * ret_dk:(h + 1) * ret_dk] = r if scale is None else r * scale

    rotary(seg("qr"), qr_ref, None)
    rotary(seg("kr"), kr_ref, kr_scale)
    vr_ref[...] = seg("vr")
    gr_ref[...] = seg("gr")
    qm_ref[...] = (seg("qm") * qm_scale).astype(qm_ref.dtype)
    gt_ref[...] = _sigmoid(seg("gt"))


def _project(x2d, g, w_packed, cos2, sin2, lay, dims, q_dtype):
    t, d = x2d.shape
    tm = min(PROJ_ROWS, t)
    n_pos_blocks = cos2.shape[0] // tm
    widths = {k: hi - lo for k, (lo, hi) in lay.items()}
    out_defs = [("qa", widths["qa"], q_dtype), ("ka", widths["ka"], F32), ("va", widths["va"], F32),
                ("qi", widths["qi"], q_dtype), ("ki", dims["idx_dim"], F32), ("wi", IDX_HEADS, F32),
                ("qr", widths["qr"], F32), ("kr", widths["kr"], F32), ("vr", widths["vr"], F32),
                ("gr", widths["gr"], F32), ("qm", widths["qm"], q_dtype), ("gt", widths["gt"], F32)]
    row = lambda w: pl.BlockSpec((tm, w), lambda i: (i, 0))
    kern = functools.partial(
        _proj_kernel, lay=lay, idx_dim=dims["idx_dim"], ret_dk=dims["ret_dk"],
        qa_scale=dims["head_dim"] ** -0.5 * LOG2E, wi_scale=(IDX_HEADS ** -0.5) * (dims["idx_dim"] ** -0.5),
        kr_scale=dims["ret_dk"] ** -0.5, qm_scale=dims["mem_dim"] ** -0.5)
    outs = pl.pallas_call(
        kern,
        grid=(t // tm,),
        in_specs=[row(d), pl.BlockSpec((1, d), lambda i: (0, 0)), _whole_vmem(),
                  pl.BlockSpec((tm, LANES), lambda i: (i % n_pos_blocks, 0)),
                  pl.BlockSpec((tm, LANES), lambda i: (i % n_pos_blocks, 0))],
        out_specs=[row(w) for _, w, _ in out_defs],
        out_shape=[jax.ShapeDtypeStruct((t, w), dt) for _, w, dt in out_defs],
        compiler_params=_params(),
        name="proj",
    )(x2d, g.reshape(1, d), w_packed, cos2, sin2)
    return {name: o for (name, _, _), o in zip(out_defs, outs)}


def _rotary_tables(pos, ret_dk, rows):
    half = ret_dk // 2
    inv = ROPE_BASE ** (-jnp.arange(half, dtype=F32) / half)
    ang = pos.astype(F32)[:, None] * inv[None, :]
    cos, sin = jnp.cos(ang), jnp.sin(ang)
    cos2 = jnp.concatenate([cos, cos], axis=1)
    sin2 = jnp.concatenate([-sin, sin], axis=1)
    reps = max(1, rows // cos2.shape[0])
    return jnp.tile(cos2, (reps, 1)), jnp.tile(sin2, (reps, 1))


def _index_scores(s4, w):
    acc = w[:, 0:1] * jnp.maximum(s4[0], 0.0)
    for h in range(1, IDX_HEADS):
        acc = acc + w[:, h:h + 1] * jnp.maximum(s4[h], 0.0)
    return jnp.where(jnp.abs(acc) < MIN_NORMAL, 0.0, acc)


def _sortable_key(bits):
    return jnp.where(bits >= 0, bits, bits ^ 0x7FFFFFFF)


def _count_chunk(rows):
    return max(LANES, COUNT_ELEMS // rows)


def _tree_sum(parts):
    while len(parts) > 1:
        parts = [a + b for a, b in zip(parts[::2], parts[1::2])] + parts[len(parts) & ~1:]
    return parts[0]


def _count_ge(keys_ref, rows, n_chunks, cand):
    cw = _count_chunk(rows)
    cb = jnp.broadcast_to(cand, (rows, LANES))

    def body(c, a):
        blk = keys_ref[:, pl.ds(pl.multiple_of(c * cw, cw), cw)]
        return a + _tree_sum([jnp.where(blk[:, u * LANES:(u + 1) * LANES] >= cb, 1.0, 0.0)
                              for u in range(cw // LANES)])

    a = lax.fori_loop(0, n_chunks, body, jnp.zeros((rows, LANES), F32), unroll=isinstance(n_chunks, int))
    return jnp.sum(a, axis=1, keepdims=True)


def _kth_largest_key(keys_ref, rows, n_chunks, topk):
    def bit_step(b, u):
        cand_u = u | lax.shift_left(jnp.int32(1), 31 - b)
        return jnp.where(_count_ge(keys_ref, rows, n_chunks, cand_u ^ INT_MIN) >= topk, cand_u, u)

    return lax.fori_loop(0, 32, bit_step, jnp.zeros((rows, 1), I32)) ^ INT_MIN


def _tie_quota(thr, n_greater, topk):
    return jnp.where(thr == INT_MIN, 0.0, topk - n_greater)


def _topk_threshold(keys_ref, rows, n_chunks, topk):
    thr = _kth_largest_key(keys_ref, rows, n_chunks, topk)
    return thr, _tie_quota(thr, _count_ge(keys_ref, rows, n_chunks, thr + 1), topk)


def _lane_top_insert(cand_ref, sc_ref, rows, width):
    depth = cand_ref.shape[0]

    def body(g, c):
        r = pl.ds(pl.multiple_of(g * 8, 8), 8)
        tops = [cand_ref[m, r, :] for m in range(depth)]
        for c0 in range(0, width, LANES):
            x = sc_ref[r, c0:c0 + LANES]
            for m in range(depth):
                tops[m], x = jnp.maximum(tops[m], x), jnp.minimum(tops[m], x)
        for m in range(depth):
            cand_ref[m, r, :] = tops[m]
        return c

    lax.fori_loop(0, rows // 8, body, 0)


def _selection_bias(keys, thr, need, run, tri):
    tw = tri.shape[0]
    eq = keys == thr
    gt = keys > thr
    eq_b = jnp.where(eq, 1.0, 0.0).astype(BF16)
    parts = []
    for c in range(keys.shape[1] // tw):
        pre = _dot(eq_b[:, c * tw:(c + 1) * tw], tri) + run
        run = pre[:, tw - 1:]
        parts.append(pre)
    prefix = parts[0] if len(parts) == 1 else jnp.concatenate(parts, axis=1)
    bias = jnp.where(gt, 0.0, jnp.where(eq, jnp.where(prefix <= need, 0.0, NEG), NEG))
    return bias, run


def _lane_tile(x, width):
    reps = width // x.shape[1]
    return x if reps == 1 else jnp.concatenate([x] * reps, axis=1)


def _softmax_tile(s, pv, m_ref, l_ref, acc_ref, idx):
    m_prev = m_ref[idx]
    m_new = jnp.maximum(m_prev, jnp.max(s, axis=1, keepdims=True))
    alpha = jnp.exp2(m_prev - m_new)
    p = jnp.exp2(s - _lane_tile(m_new, s.shape[1]))
    l_ref[idx] = alpha * l_ref[idx] + jnp.sum(p, axis=1, keepdims=True)
    acc = acc_ref[idx]
    acc_ref[idx] = _lane_tile(alpha, acc.shape[1]) * acc + pv(p.astype(BF16))
    m_ref[idx] = m_new


def _sparse_prompt_kernel(qi_ref, wi_ref, qa_ref, kit_ref, kt_ref, v_ref, tri_ref, o_ref,
                          keys_ref, sc_ref, cand_ref, ckeys_ref, p_ref, m_ref, acc_ref,
                          *, tq, kb, topk, kv_heads, group, head_dim):
    i = pl.program_id(1)
    n_tiles = ((i + 1) * tq + kb - 1) // kb
    row_pos = i * tq + lax.broadcasted_iota(I32, (tq, kb), 0)
    col_iota = lax.broadcasted_iota(I32, (tq, kb), 1)
    w = wi_ref[...]
    qi = qi_ref[...].reshape(IDX_HEADS * tq, qi_ref.shape[2])
    cw = _count_chunk(tq)
    n_chunks = n_tiles * (kb // cw)
    depth = cand_ref.shape[0]
    cand_chunks = depth * LANES // cw
    prefilter = n_chunks >= PREFILTER_MIN_CHUNKS

    @pl.when(prefilter)
    def _():
        cand_ref[...] = jnp.full(cand_ref.shape, -jnp.inf, F32)

    def score_tile(j, c, diagonal):
        k0 = pl.multiple_of(j * kb, kb)
        s4 = _dot(qi, kit_ref[0, :, pl.ds(k0, kb)]).reshape(IDX_HEADS, tq, kb)
        sc = _index_scores(s4, w)
        key = _sortable_key(lax.bitcast_convert_type(sc, I32))
        if diagonal:
            visible = k0 + col_iota <= row_pos
            key = jnp.where(visible, key, INT_MIN)
            sc = jnp.where(visible, sc, -jnp.inf)
        keys_ref[:, pl.ds(k0, kb)] = key

        @pl.when(prefilter)
        def _():
            sc_ref[...] = sc
            _lane_top_insert(cand_ref, sc_ref, tq, kb)

        return c

    lax.fori_loop(0, n_tiles - 1, functools.partial(score_tile, diagonal=False), 0)
    score_tile(n_tiles - 1, 0, diagonal=True)

    def ranked_directly():
        return _topk_threshold(keys_ref, tq, n_chunks, topk)

    def ranked_from_lane_tops():
        for m in range(depth):
            ckeys_ref[:, m * LANES:(m + 1) * LANES] = _sortable_key(lax.bitcast_convert_type(cand_ref[m], I32))
        thr = _kth_largest_key(ckeys_ref, tq, cand_chunks, topk)
        n_greater = _count_ge(keys_ref, tq, n_chunks, thr + 1)
        kept = _count_ge(ckeys_ref, tq, cand_chunks, thr + 1)
        missed = jnp.max(n_greater - kept)
        return lax.cond(missed == 0.0, lambda: (thr, _tie_quota(thr, n_greater, topk)), ranked_directly)

    thr, need = lax.cond(prefilter, ranked_from_lane_tops, ranked_directly)

    m_ref[...] = jnp.full(m_ref.shape, NEG, F32)
    acc_ref[...] = jnp.zeros(acc_ref.shape, F32)

    def attn_tile(j, run):
        k0 = pl.multiple_of(j * kb, kb)
        bias, run = _selection_bias(keys_ref[:, pl.ds(k0, kb)], thr, need, run, tri_ref[...])
        gr = group * tq
        for pair in range(kv_heads // 2):
            alphas = []
            for half in range(2):
                n = 2 * pair + half
                qn = qa_ref[n * group:(n + 1) * group].reshape(gr, head_dim)
                s = _dot(qn, kt_ref[0, n * head_dim:(n + 1) * head_dim, pl.ds(k0, kb)])
                s = (s.reshape(group, tq, kb) + bias[None]).reshape(gr, kb)
                m_prev = m_ref[n]
                m_new = jnp.maximum(m_prev, jnp.max(s, axis=1, keepdims=True))
                alphas.append(jnp.exp2(m_prev - m_new))
                p_ref[half * gr:(half + 1) * gr, :] = jnp.exp2(s - _lane_tile(m_new, kb)).astype(BF16)
                m_ref[n] = m_new
            pv = _dot(p_ref[...], v_ref[0, pl.ds(k0, kb), pair * 2 * LANES:(pair + 1) * 2 * LANES])
            for half in range(2):
                n = 2 * pair + half
                acc_ref[n] = alphas[half] * acc_ref[n] + pv[half * gr:(half + 1) * gr, half * LANES:(half + 1) * LANES]
        return run

    lax.fori_loop(0, n_tiles, attn_tile, jnp.zeros((tq, 1), F32))

    for n in range(kv_heads):
        acc = acc_ref[n]
        o = acc * pltpu.roll(1.0 / acc, LANES - head_dim, 1)
        for g in range(group):
            h = n * group + g
            o_ref[:, h * head_dim:(h + 1) * head_dim] = o[g * tq:(g + 1) * tq, :head_dim]


def _sparse_attention_prompt(qi, wi, qa, kit, kt, v, tri, topk, kv_heads, head_dim):
    q_heads = qa.shape[0]
    b, s, _ = v.shape
    tq, kb = min(ATTN_TQ, s), min(ATTN_KB, s)
    nq = s // tq
    group = q_heads // kv_heads
    kern = functools.partial(_sparse_prompt_kernel, tq=tq, kb=kb, topk=topk, kv_heads=kv_heads,
                             group=group, head_dim=head_dim)
    resident = lambda shape: pl.BlockSpec((1,) + shape, lambda bi, i: (bi, 0, 0), pipeline_mode=pl.Buffered(1))
    return pl.pallas_call(
        kern,
        grid=(b, nq),
        in_specs=[pl.BlockSpec((IDX_HEADS, tq, qi.shape[2]), lambda bi, i: (0, bi * nq + i, 0)),
                  pl.BlockSpec((tq, IDX_HEADS), lambda bi, i: (bi * nq + i, 0)),
                  pl.BlockSpec((q_heads, tq, head_dim), lambda bi, i: (0, bi * nq + i, 0)),
                  resident(kit.shape[1:]), resident(kt.shape[1:]), resident(v.shape[1:]), _whole_vmem()],
        out_specs=pl.BlockSpec((tq, q_heads * head_dim), lambda bi, i: (bi * nq + i, 0)),
        out_shape=jax.ShapeDtypeStruct((b * s, q_heads * head_dim), F32),
        scratch_shapes=[pltpu.VMEM((tq, s), I32),
                        pltpu.VMEM((tq, kb), F32),
                        pltpu.VMEM((PREFILTER_DEPTH, tq, LANES), F32),
                        pltpu.VMEM((tq, PREFILTER_DEPTH * LANES), I32),
                        pltpu.VMEM((2 * group * tq, kb), BF16),
                        pltpu.VMEM((kv_heads, group * tq, LANES), F32),
                        pltpu.VMEM((kv_heads, group * tq, LANES), F32)],
        compiler_params=_params(2),
        name="sparse_attn_prompt",
    )(qi, wi, qa, kit, kt, v, tri)


def _sparse_sample_kernel(pt_ref, qi_ref, wi_ref, qbd_ref, kin_ref, kan_ref, van_ref, tri_ref,
                          cidx_hbm, ck_hbm, cv_hbm, o_ref,
                          ibuf, kbuf, vbuf, sems, keys_ref, m_ref, l_ref, acc_ref,
                          *, t, kb, topk, n_pages, page, kv_heads, group, head_dim):
    b = pl.program_id(0)
    nb = pl.num_programs(0)
    past = n_pages * page
    rows = kv_heads * group * t
    cw = _count_chunk(t)

    def page_copies(bb, slot, p):
        pg = pt_ref[bb, p]
        dst = pl.ds(pl.multiple_of(p * page, page), page)
        return (pltpu.make_async_copy(cidx_hbm.at[pg], ibuf.at[slot, :, dst], sems.at[slot, 0]),
                pltpu.make_async_copy(ck_hbm.at[pg], kbuf.at[slot, :, dst], sems.at[slot, 1]),
                pltpu.make_async_copy(cv_hbm.at[pg], vbuf.at[slot, :, dst], sems.at[slot, 2]))

    def start_fetch(bb, slot):
        def body(p, c):
            for cp in page_copies(bb, slot, p):
                cp.start()
            return c
        lax.fori_loop(0, n_pages, body, 0)

    def wait_fetch(bb, slot):
        def body(p, c):
            for cp in page_copies(bb, slot, p):
                cp.wait()
            return c
        lax.fori_loop(0, n_pages, body, 0)

    slot = b % 2

    @pl.when(b == 0)
    def _():
        start_fetch(b, slot)

    @pl.when(b + 1 < nb)
    def _():
        start_fetch(b + 1, 1 - slot)

    wait_fetch(b, slot)

    w = wi_ref[0]
    qi = qi_ref[0]
    n_tiles = past // kb

    def score_tile(j, c):
        k0 = pl.multiple_of(j * kb, kb)
        s4 = _dot(qi, ibuf[slot, :, pl.ds(k0, kb)].astype(BF16)).reshape(IDX_HEADS, t, kb)
        keys_ref[:, pl.ds(k0, kb)] = _sortable_key(lax.bitcast_convert_type(_index_scores(s4, w), I32))
        return c

    lax.fori_loop(0, n_tiles, score_tile, 0)
    s4 = _dot_nt(qi, kin_ref[0]).reshape(IDX_HEADS, t, LANES)
    key_new = _sortable_key(lax.bitcast_convert_type(_index_scores(s4, w), I32))
    new_col = lax.broadcasted_iota(I32, (t, LANES), 1)
    new_row = lax.broadcasted_iota(I32, (t, LANES), 0)
    keys_ref[:, past:past + LANES] = jnp.where(new_col <= new_row, key_new, INT_MIN)
    keys_ref[:, past + LANES:past + cw] = jnp.full((t, cw - LANES), INT_MIN, I32)

    thr, need = _topk_threshold(keys_ref, t, past // cw + 1, topk)

    m_ref[...] = jnp.full(m_ref.shape, NEG, F32)
    l_ref[...] = jnp.zeros(l_ref.shape, F32)
    acc_ref[...] = jnp.zeros(acc_ref.shape, F32)
    qbd = qbd_ref[0]

    def tile_bias(bias, width):
        return jnp.broadcast_to(bias[None], (rows // t, t, width)).reshape(rows, width)

    def attn_tile(j, run):
        k0 = pl.multiple_of(j * kb, kb)
        bias, run = _selection_bias(keys_ref[:, pl.ds(k0, kb)], thr, need, run, tri_ref[...])
        s = _dot(qbd, kbuf[slot, :, pl.ds(k0, kb)].astype(BF16)) + tile_bias(bias, kb)
        vt = vbuf[slot, :, pl.ds(k0, kb)].astype(BF16)
        _softmax_tile(s, lambda p: _dot_nt(p, vt), m_ref, l_ref, acc_ref, 0)
        return run

    run = lax.fori_loop(0, n_tiles, attn_tile, jnp.zeros((t, 1), F32))
    bias, _ = _selection_bias(keys_ref[:, past:past + LANES], thr, need, run, tri_ref[:LANES, :LANES])
    s = _dot_nt(qbd, kan_ref[0]) + tile_bias(bias, LANES)
    van = van_ref[0]
    _softmax_tile(s, lambda p: _dot(p, van), m_ref, l_ref, acc_ref, 0)

    o = acc_ref[0] * (1.0 / _lane_tile(l_ref[0], acc_ref.shape[2]))
    for n in range(kv_heads):
        for g in range(group):
            h = n * group + g
            r0 = (n * group + g) * t
            o_ref[0, :, h * head_dim:(h + 1) * head_dim] = o[r0:r0 + t, n * head_dim:(n + 1) * head_dim]


def _sparse_attention_sample(page_table, qi, wi, qbd, ki_new, ka_new, va_new, tri, cache_idx_t, cache_k_t, cache_v_t,
                             topk, kv_heads, group, head_dim):
    db, t, _ = wi.shape
    n_pages = page_table.shape[1]
    page = cache_idx_t.shape[2]
    past = n_pages * page
    kb = min(SAMPLE_KB, past)
    rows = kv_heads * group * t
    kvw = kv_heads * head_dim
    cw = _count_chunk(t)
    kern = functools.partial(_sparse_sample_kernel, t=t, kb=kb, topk=topk, n_pages=n_pages, page=page,
                             kv_heads=kv_heads, group=group, head_dim=head_dim)
    per_b = lambda shape: pl.BlockSpec((1,) + shape, lambda b, pt: (b, 0, 0))
    grid_spec = pltpu.PrefetchScalarGridSpec(
        num_scalar_prefetch=1,
        grid=(db,),
        in_specs=[per_b(qi.shape[1:]), per_b(wi.shape[1:]), per_b(qbd.shape[1:]), per_b(ki_new.shape[1:]),
                  per_b(ka_new.shape[1:]), per_b(va_new.shape[1:]),
                  pl.BlockSpec(tri.shape, lambda b, pt: (0, 0)),
                  pl.BlockSpec(memory_space=pl.ANY), pl.BlockSpec(memory_space=pl.ANY),
                  pl.BlockSpec(memory_space=pl.ANY)],
        out_specs=per_b((t, kv_heads * group * head_dim)),
        scratch_shapes=[pltpu.VMEM((2, cache_idx_t.shape[1], past), F32),
                        pltpu.VMEM((2, kvw, past), F32),
                        pltpu.VMEM((2, kvw, past), F32),
                        pltpu.SemaphoreType.DMA((2, 3)),
                        pltpu.VMEM((t, past + cw), I32),
                        pltpu.VMEM((1, rows, LANES), F32),
                        pltpu.VMEM((1, rows, LANES), F32),
                        pltpu.VMEM((1, rows, kvw), F32)])
    return pl.pallas_call(
        kern,
        grid_spec=grid_spec,
        out_shape=jax.ShapeDtypeStruct((db, t, kv_heads * group * head_dim), F32),
        compiler_params=_params(),
        name="sparse_attn_sample",
    )(page_table, qi, wi, qbd, ki_new, ka_new, va_new, tri, cache_idx_t, cache_k_t, cache_v_t)


def _log_gamma(heads):
    return np.log(1.0 - np.exp2(-RET_DECAY_OFFSET - np.arange(heads, dtype=np.float64)))


def _retention_tables(heads, chunk, dk, group):
    lg = _log_gamma(heads)
    i = np.arange(chunk, dtype=np.float64)
    diff = i[:, None] - i[None, :]
    decay = np.where(diff >= 0, np.exp(lg[:, None, None] * np.maximum(diff, 0.0)), 0.0)
    decay_bd = np.zeros((heads, group * chunk, group * chunk))
    for gidx in range(group):
        decay_bd[:, gidx * chunk:(gidx + 1) * chunk, gidx * chunk:(gidx + 1) * chunk] = decay
    q_scale = np.tile(np.exp(lg[:, None] * (i + 1.0)), (1, group))
    k_scale = np.tile(np.exp(lg[:, None] * (chunk - 1.0 - i)), (1, group))
    state_scale = np.exp(lg * chunk)
    rows_q = np.broadcast_to(q_scale[:, :, None], (heads, group * chunk, dk))
    rows_k = np.broadcast_to(k_scale[:, :, None], (heads, group * chunk, dk))
    cols_k = np.broadcast_to(k_scale[:, None, :], (heads, dk, group * chunk))
    as32 = lambda a: jnp.asarray(np.ascontiguousarray(a), F32)
    return as32(decay_bd), as32(rows_q), as32(rows_k), as32(cols_k), [float(v) for v in state_scale]


def _groupnorm_gate(o, gn_g, g_r):
    mu = jnp.mean(o, axis=-1, keepdims=True)
    d = o - mu
    var = jnp.mean(d * d, axis=-1, keepdims=True)
    return (g_r * _sigmoid(g_r)) * ((d * lax.rsqrt(var + EPS)) * gn_g)


def _retention_prompt_kernel(q_ref, k_ref, v_ref, g_ref, gn_ref, dec_ref, qs_ref, ks_ref, o_ref, st_ref, state,
                             *, heads, dk, dv, state_scale):
    c = pl.program_id(0)

    @pl.when(c == 0)
    def _():
        state[...] = jnp.zeros(state.shape, F32)

    for bi in range(q_ref.shape[0]):
        for h in range(heads):
            q = q_ref[bi, :, h * dk:(h + 1) * dk]
            k = k_ref[bi, :, h * dk:(h + 1) * dk]
            v = v_ref[bi, :, h * dv:(h + 1) * dv].astype(BF16)
            s_old = state[bi, h]
            inner = _dot_nt(q.astype(BF16), k.astype(BF16)) * dec_ref[h]
            o = _dot(inner.astype(BF16), v) + _dot((q * qs_ref[h]).astype(BF16), s_old.astype(BF16))
            state[bi, h] = state_scale[h] * s_old + _dot((k * ks_ref[h]).T.astype(BF16), v)
            o_ref[bi, :, h * dv:(h + 1) * dv] = _groupnorm_gate(o, gn_ref[:, h * dv:(h + 1) * dv],
                                                                g_ref[bi, :, h * dv:(h + 1) * dv])

    @pl.when(c == pl.num_programs(0) - 1)
    def _():
        st_ref[...] = state[...]


def _retention_prompt(q, k, v, g, gn_g, heads, dk, dv):
    b, s, _ = q.shape
    chunk = min(RET_CHUNK, s)
    dec, qs, ks, _, state_scale = _retention_tables(heads, chunk, dk, 1)
    blk = lambda w: pl.BlockSpec((b, chunk, w), lambda ci: (0, ci, 0))
    const = lambda a: pl.BlockSpec(a.shape, lambda ci: (0,) * a.ndim)
    gn2 = gn_g.reshape(1, heads * dv)
    kern = functools.partial(_retention_prompt_kernel, heads=heads, dk=dk, dv=dv, state_scale=state_scale)
    return pl.pallas_call(
        kern,
        grid=(s // chunk,),
        in_specs=[blk(heads * dk), blk(heads * dk), blk(heads * dv), blk(heads * dv),
                  const(gn2), const(dec), const(qs), const(ks)],
        out_specs=[blk(heads * dv), pl.BlockSpec((b, heads, dk, dv), lambda ci: (0, 0, 0, 0))],
        out_shape=[jax.ShapeDtypeStruct((b, s, heads * dv), F32),
                   jax.ShapeDtypeStruct((b, heads, dk, dv), F32)],
        scratch_shapes=[pltpu.VMEM((b, heads, dk, dv), F32)],
        compiler_params=_params(),
        name="retention_prompt",
    )(q, k, v, g, gn2, dec, qs, ks)


def _retention_sample_kernel(q_ref, k_ref, kt_ref, v_ref, g_ref, gn_ref, dec_ref, qs_ref, kst_ref, st_ref,
                             o_ref, sto_ref, *, heads, dk, dv, t, group, state_scale):
    rows = group * t
    col_batch = lax.broadcasted_iota(I32, (dk, rows), 1) // t
    for h in range(heads):
        q = q_ref[:, h * dk:(h + 1) * dk]
        k = k_ref[:, h * dk:(h + 1) * dk]
        v = v_ref[:, h * dv:(h + 1) * dv].astype(BF16)
        inner = _dot_nt(q.astype(BF16), k.astype(BF16)) * dec_ref[h]
        o = _dot(inner.astype(BF16), v)
        qd = (q * qs_ref[h]).astype(BF16)
        kdt = kt_ref[0, h] * kst_ref[h]
        cross = []
        for bi in range(group):
            s_old = st_ref[bi, h]
            cross.append(_dot(qd, s_old.astype(BF16))[bi * t:(bi + 1) * t])
            kd_b = jnp.where(col_batch == bi, kdt, 0.0).astype(BF16)
            sto_ref[bi, h] = state_scale[h] * s_old + _dot(kd_b, v)
        o = o + jnp.concatenate(cross, axis=0)
        o_ref[:, h * dv:(h + 1) * dv] = _groupnorm_gate(o, gn_ref[:, h * dv:(h + 1) * dv],
                                                        g_ref[:, h * dv:(h + 1) * dv])


def _retention_sample(state, q, k, v, g, gn_g, heads, dk, dv, t):
    db = state.shape[0]
    group = min(SAMPLE_GROUP, db)
    rows = group * t
    dec, qs, _, kst, state_scale = _retention_tables(heads, t, dk, group)
    kt = k.reshape(db // group, rows, heads, dk).transpose(0, 2, 3, 1)
    gn2 = gn_g.reshape(1, heads * dv)
    blk = lambda w: pl.BlockSpec((rows, w), lambda i: (i, 0))
    const = lambda a: pl.BlockSpec(a.shape, lambda i: (0,) * a.ndim)
    st_spec = pl.BlockSpec((group, heads, dk, dv), lambda i: (i, 0, 0, 0))
    kern = functools.partial(_retention_sample_kernel, heads=heads, dk=dk, dv=dv, t=t, group=group,
                             state_scale=state_scale)
    return pl.pallas_call(
        kern,
        grid=(db // group,),
        in_specs=[blk(heads * dk), blk(heads * dk),
                  pl.BlockSpec((1, heads, dk, rows), lambda i: (i, 0, 0, 0)),
                  blk(heads * dv), blk(heads * dv), const(gn2), const(dec), const(qs), const(kst), st_spec],
        out_specs=[blk(heads * dv), st_spec],
        out_shape=[jax.ShapeDtypeStruct((db * t, heads * dv), F32),
                   jax.ShapeDtypeStruct(state.shape, F32)],
        compiler_params=_params(),
        name="retention_sample",
    )(q, k, kt, v, g, gn2, dec, qs, kst, state)


def _mem_kv_kernel(x_ref, g_ref, w_ref, k_ref, v_ref):
    hb = _rmsnorm(x_ref[...], g_ref[...]).astype(BF16)
    half = w_ref.shape[1] // 2
    k_ref[...] = _dot(hb, w_ref[:, :half])
    v_ref[...] = _dot(hb, w_ref[:, half:])


def _memory_kv(mem2d, g, w_bf16):
    t, d = mem2d.shape
    half = w_bf16.shape[1] // 2
    tm = min(PROJ_ROWS, t)
    return pl.pallas_call(
        _mem_kv_kernel,
        grid=(t // tm,),
        in_specs=[pl.BlockSpec((tm, d), lambda i: (i, 0)), pl.BlockSpec((1, d), lambda i: (0, 0)), _whole_vmem()],
        out_specs=[pl.BlockSpec((tm, half), lambda i: (i, 0))] * 2,
        out_shape=[jax.ShapeDtypeStruct((t, half), F32)] * 2,
        compiler_params=_params(),
        name="memory_kv",
    )(mem2d, g.reshape(1, d), w_bf16)


def _mem_head_attention(q, k, v):
    s = _dot_nt(q, k)
    p = jnp.exp(s - jnp.max(s, axis=1, keepdims=True))
    return _dot(p.astype(BF16), v) * (1.0 / jnp.sum(p, axis=1, keepdims=True))


def _mem_attn_prompt_kernel(q_ref, k_ref, v_ref, o_ref, *, heads, dim):
    for h in range(heads):
        sl = slice(h * dim, (h + 1) * dim)
        o_ref[0, :, sl] = _mem_head_attention(q_ref[0, :, sl], k_ref[0, :, sl], v_ref[0, :, sl])


def _memory_attention_prompt(q, mk, mv, heads, dim):
    b, s, w = q.shape
    tm = min(MEM_ROWS, s)
    m = mk.shape[1]
    return pl.pallas_call(
        functools.partial(_mem_attn_prompt_kernel, heads=heads, dim=dim),
        grid=(b, s // tm),
        in_specs=[pl.BlockSpec((1, tm, w), lambda bi, i: (bi, i, 0)),
                  pl.BlockSpec((1, m, w), lambda bi, i: (bi, 0, 0)),
                  pl.BlockSpec((1, m, w), lambda bi, i: (bi, 0, 0))],
        out_specs=pl.BlockSpec((1, tm, w), lambda bi, i: (bi, i, 0)),
        out_shape=jax.ShapeDtypeStruct((b, s, w), F32),
        compiler_params=_params(2),
        name="memory_attn_prompt",
    )(q, mk, mv)


def _mem_attn_sample_kernel(q_ref, k_ref, v_ref, o_ref, *, heads, dim, t, group):
    for bi in range(group):
        for h in range(heads):
            sl = slice(h * dim, (h + 1) * dim)
            o = _mem_head_attention(q_ref[:, sl].astype(BF16), k_ref[bi, :, sl], v_ref[bi, :, sl])
            o_ref[bi * t:(bi + 1) * t, sl] = o[bi * t:(bi + 1) * t]


def _memory_attention_sample(q, mk, mv, heads, dim, t):
    db, m, w = mk.shape
    group = min(SAMPLE_GROUP, db)
    mem_spec = pl.BlockSpec((group, m, w), lambda i: (i, 0, 0))
    return pl.pallas_call(
        functools.partial(_mem_attn_sample_kernel, heads=heads, dim=dim, t=t, group=group),
        grid=(db // group,),
        in_specs=[pl.BlockSpec((group * t, w), lambda i: (i, 0)), mem_spec, mem_spec],
        out_specs=pl.BlockSpec((group * t, w), lambda i: (i, 0)),
        out_shape=jax.ShapeDtypeStruct((db * t, w), F32),
        compiler_params=_params(),
        name="memory_attn_sample",
    )(q, mk, mv)


def _merge_ffn_kernel(x_ref, gt_ref, oa_ref, or_ref, om_ref, wo_ref, fg_ref, wg_ref, wu_ref, wd_ref, fin_ref,
                      y_ref, *, final_norm):
    d = x_ref.shape[1]
    u = (gt_ref[:, :d] * oa_ref[...] + gt_ref[:, d:2 * d] * or_ref[...] + gt_ref[:, 2 * d:] * om_ref[...])
    x1 = x_ref[...] + _dot(u.astype(BF16), wo_ref[...])
    hb = _rmsnorm(x1, fg_ref[...]).astype(BF16)
    a = _dot(hb, wg_ref[...])
    act = (a * _sigmoid(a)) * _dot(hb, wu_ref[...])
    x2 = x1 + _dot(act.astype(BF16), wd_ref[...])
    y_ref[...] = _rmsnorm(x2, fin_ref[...]) if final_norm else x2


def _merge_ffn(x2d, gates, o_a, o_r, o_m, w_out, ffn_g, w_gate, w_up, w_down, final_g, final_norm):
    t, d = x2d.shape
    tm = min(PROJ_ROWS, t)
    row = lambda w: pl.BlockSpec((tm, w), lambda i: (i, 0))
    vec = pl.BlockSpec((1, d), lambda i: (0, 0))
    return pl.pallas_call(
        functools.partial(_merge_ffn_kernel, final_norm=final_norm),
        grid=(t // tm,),
        in_specs=[row(d), row(gates.shape[1]), row(d), row(d), row(d), _whole_vmem(), vec,
                  _whole_vmem(), _whole_vmem(), _whole_vmem(), vec],
        out_specs=row(d),
        out_shape=jax.ShapeDtypeStruct((t, d), F32),
        compiler_params=_params(),
        name="merge_ffn",
    )(x2d, gates, o_a, o_r, o_m, w_out, ffn_g.reshape(1, d), w_gate, w_up, w_down, final_g.reshape(1, d))


def kernel(x_prompt, x_sample, cache_k, cache_v, cache_idx_k, state_ret, cache_mem_k, cache_mem_v, page_table,
           mem_prompt, attn_norm_g, w_in, ret_gn_g, w_out, ffn_norm_g, w_gate, w_up, w_down, mem_norm_g, w_mem_kv,
           final_norm_g):
    depth = w_in.shape[0]
    b, s, d = x_prompt.shape
    db, t, _ = x_sample.shape
    _, n_pool, page, kv_heads, head_dim = cache_k.shape
    idx_dim = cache_idx_k.shape[-1]
    _, _, ret_heads, ret_dk, ret_dv = state_ret.shape
    _, _, mem_tokens, mem_heads, mem_dim = cache_mem_k.shape
    q_heads = d // head_dim
    group = q_heads // kv_heads
    kvw = kv_heads * head_dim
    n_pages = page_table.shape[1]
    past = n_pages * page
    dims = dict(idx_dim=idx_dim, ret_dk=ret_dk, head_dim=head_dim, mem_dim=mem_dim)
    lay, _ = _proj_layout(d, kvw, idx_dim, ret_heads * ret_dk, ret_heads * ret_dv, mem_heads * mem_dim,
                          w_in.shape[2] - (d + 2 * kvw + IDX_HEADS * idx_dim + idx_dim + IDX_HEADS
                                           + 2 * ret_heads * ret_dk + 2 * ret_heads * ret_dv + mem_heads * mem_dim))
    o_ki = d + 2 * kvw + IDX_HEADS * idx_dim
    o_qr = o_ki + idx_dim + IDX_HEADS

    cos_p, sin_p = _rotary_tables(jnp.arange(s), ret_dk, min(PROJ_ROWS, b * s))
    cos_s, sin_s = _rotary_tables(past + jnp.arange(t), ret_dk, min(PROJ_ROWS, db * t))
    tri = (jnp.arange(TRI_W)[:, None] <= jnp.arange(TRI_W)[None, :]).astype(BF16)
    topk_p = min(TOPK_MAX, s // 4)
    topk_s = min(TOPK_MAX, (past + t) // 4)

    xp = x_prompt.reshape(b * s, d)
    xs = x_sample.reshape(db * t, d)
    outs = {k: [] for k in ("k_p", "v_p", "ik_p", "rs_p", "mk_p", "mv_p", "k_s", "v_s", "ik_s", "rs_s")}
    for l in range(depth):
        last = l == depth - 1
        wl = w_in[l]
        w_packed = jnp.concatenate(
            [wl[:, :o_ki], wl[:, o_ki:o_qr], jnp.zeros((d, LANES - (o_qr - o_ki)), wl.dtype), wl[:, o_qr:]],
            axis=1).astype(BF16)
        w_out_b, w_gate_b, w_up_b, w_down_b = (a[l].astype(BF16) for a in (w_out, w_gate, w_up, w_down))

        p = _project(xp, attn_norm_g[l], w_packed, cos_p, sin_p, lay, dims, BF16)
        qi = p["qi"].reshape(b * s, IDX_HEADS, idx_dim).transpose(1, 0, 2)
        qa = p["qa"].reshape(b * s, q_heads, head_dim).transpose(1, 0, 2)
        kit = p["ki"].astype(BF16).reshape(b, s, idx_dim).transpose(0, 2, 1)
        kt = p["ka"].astype(BF16).reshape(b, s, kvw).transpose(0, 2, 1)
        va = p["va"].astype(BF16).reshape(b, s, kv_heads, head_dim)
        vb = jnp.concatenate([va, jnp.ones((b, s, kv_heads, LANES - head_dim), BF16)], axis=-1)
        o_a = _sparse_attention_prompt(qi, p["wi"], qa, kit, kt, vb.reshape(b, s, kv_heads * LANES), tri, topk_p,
                                       kv_heads, head_dim)
        o_r, st_p = _retention_prompt(p["qr"].reshape(b, s, -1), p["kr"].reshape(b, s, -1),
                                      p["vr"].reshape(b, s, -1), p["gr"].reshape(b, s, -1), ret_gn_g[l],
                                      ret_heads, ret_dk, ret_dv)
        mk, mv = _memory_kv(mem_prompt.reshape(b * mem_tokens, d), mem_norm_g[l], w_mem_kv[l].astype(BF16))
        o_m = _memory_attention_prompt(p["qm"].reshape(b, s, -1), mk.astype(BF16).reshape(b, mem_tokens, -1),
                                       mv.astype(BF16).reshape(b, mem_tokens, -1), mem_heads, mem_dim)
        xp = _merge_ffn(xp, p["gt"], o_a, o_r.reshape(b * s, -1), o_m.reshape(b * s, -1), w_out_b, ffn_norm_g[l],
                        w_gate_b, w_up_b, w_down_b, final_norm_g, last)
        outs["k_p"].append(p["ka"].reshape(b, s, kv_heads, head_dim))
        outs["v_p"].append(p["va"].reshape(b, s, kv_heads, head_dim))
        outs["ik_p"].append(p["ki"].reshape(b, s, idx_dim))
        outs["rs_p"].append(st_p)
        outs["mk_p"].append(mk.reshape(b, mem_tokens, mem_heads, mem_dim))
        outs["mv_p"].append(mv.reshape(b, mem_tokens, mem_heads, mem_dim))

        q = _project(xs, attn_norm_g[l], w_packed, cos_s, sin_s, lay, dims, F32)
        qi_s = q["qi"].reshape(db, t, IDX_HEADS, idx_dim).transpose(0, 2, 1, 3).reshape(db, IDX_HEADS * t, idx_dim)
        qa_s = q["qa"].reshape(db, t, kv_heads, group, head_dim).transpose(0, 2, 3, 1, 4)
        qbd = (qa_s.reshape(db, kv_heads, group * t, 1, head_dim)
               * jnp.eye(kv_heads, dtype=F32)[None, :, None, :, None]).reshape(db, kv_heads * group * t, kvw)
        pad_rows = lambda a: jnp.pad(a.reshape(db, t, -1), ((0, 0), (0, LANES - t), (0, 0))).astype(BF16)
        o_a = _sparse_attention_sample(
            page_table, qi_s.astype(BF16), q["wi"].reshape(db, t, IDX_HEADS), qbd.astype(BF16),
            pad_rows(q["ki"]), pad_rows(q["ka"]), pad_rows(q["va"]), tri,
            jnp.swapaxes(cache_idx_k[l], 1, 2),
            cache_k[l].transpose(0, 2, 3, 1).reshape(n_pool, kvw, page),
            cache_v[l].transpose(0, 2, 3, 1).reshape(n_pool, kvw, page),
            topk_s, kv_heads, group, head_dim).reshape(db * t, d)
        o_r, st_s = _retention_sample(state_ret[l], q["qr"], q["kr"], q["vr"], q["gr"], ret_gn_g[l],
                                      ret_heads, ret_dk, ret_dv, t)
        o_m = _memory_attention_sample(q["qm"], cache_mem_k[l].reshape(db, mem_tokens, -1).astype(BF16),
                                       cache_mem_v[l].reshape(db, mem_tokens, -1).astype(BF16), mem_heads, mem_dim, t)
        xs = _merge_ffn(xs, q["gt"], o_a, o_r, o_m, w_out_b, ffn_norm_g[l], w_gate_b, w_up_b, w_down_b,
                        final_norm_g, last)
        outs["k_s"].append(q["ka"].reshape(db, t, kv_heads, head_dim))
        outs["v_s"].append(q["va"].reshape(db, t, kv_heads, head_dim))
        outs["ik_s"].append(q["ki"].reshape(db, t, idx_dim))
        outs["rs_s"].append(st_s)

    stk = lambda k: jnp.stack(outs[k])
    return (xp.reshape(b, s, d), xs.reshape(db, t, d), stk("k_p"), stk("v_p"), stk("ik_p"), stk("rs_p"),
            stk("mk_p"), stk("mv_p"), stk("k_s"), stk("v_s"), stk("ik_s"), stk("rs_s"))
```

```python
import functools
import math

import numpy as np
import jax
import jax.numpy as jnp
from jax import lax
from jax.experimental import pallas as pl
from jax.experimental.pallas import tpu as pltpu

F32 = jnp.float32
BF16 = jnp.bfloat16
I32 = jnp.int32

IDX_HEADS = 4
TOPK_MAX = 256
RET_CHUNK = 128
RET_DECAY_OFFSET = 5.0
ROPE_BASE = 10000.0
EPS = 1e-6

LANES = 128
VMEM_LIMIT_BYTES = 56 * 1024 * 1024

NEG = -1e30
INT_MIN = -(2 ** 31)
LOG2E = math.log2(math.e)
MIN_NORMAL = 2.0 ** -126

PROJ_ROWS = 256
ATTN_TQ = 128
ATTN_KB = 1024
SAMPLE_KB = 1024
COUNT_ELEMS = 65536
TRI_W = 256
PREFILTER_DEPTH = 12
PREFILTER_MIN_CHUNKS = 8
MEM_ROWS = 512
SAMPLE_GROUP = 8


def _params(n_axes=1):
    return pltpu.CompilerParams(dimension_semantics=("arbitrary",) * n_axes,
                                vmem_limit_bytes=VMEM_LIMIT_BYTES)


def _whole_vmem():
    return pl.BlockSpec(memory_space=pltpu.VMEM)


def _rmsnorm(x, g):
    return (x * lax.rsqrt(jnp.mean(x * x, axis=-1, keepdims=True) + EPS)) * g


def _sigmoid(x):
    return 1.0 / (1.0 + jnp.exp(-x))


def _dot(a, b):
    return jnp.dot(a, b, preferred_element_type=F32)


def _dot_nt(a, b):
    return lax.dot_general(a, b, (((1,), (1,)), ((), ())), preferred_element_type=F32)


def _proj_layout(d_model, kv_w, idx_dim, ret_qk_w, ret_v_w, mem_w, n_gate):
    segs = [("qa", d_model), ("ka", kv_w), ("va", kv_w), ("qi", IDX_HEADS * idx_dim), ("kw", LANES),
            ("qr", ret_qk_w), ("kr", ret_qk_w), ("vr", ret_v_w), ("gr", ret_v_w), ("qm", mem_w),
            ("gt", n_gate)]
    out, off = {}, 0
    for name, w in segs:
        out[name] = (off, off + w)
        off += w
    return out, off


def _proj_kernel(x_ref, g_ref, w_ref, cos_ref, sin_ref,
                 qa_ref, ka_ref, va_ref, qi_ref, ki_ref, wi_ref, qr_ref, kr_ref, vr_ref, gr_ref, qm_ref, gt_ref,
                 *, lay, idx_dim, ret_dk, qa_scale, wi_scale, kr_scale, qm_scale):
    hb = _rmsnorm(x_ref[...], g_ref[...]).astype(BF16)

    def seg(name):
        lo, hi = lay[name]
        return _dot(hb, w_ref[:, lo:hi])

    qa_ref[...] = (seg("qa") * qa_scale).astype(qa_ref.dtype)
    ka_ref[...] = seg("ka")
    va_ref[...] = seg("va")
    qi_ref[...] = seg("qi").astype(qi_ref.dtype)
    kw = seg("kw")
    ki_ref[...] = kw[:, :idx_dim]
    wi_ref[...] = kw[:, idx_dim:idx_dim + IDX_HEADS] * wi_scale

    cos2 = cos_ref[...]
    sin2 = sin_ref[...]

    def rotary(z, out_ref, scale):
        for h in range(z.shape[1] // ret_dk):
            zh = z[:, h * ret_dk:(h + 1) * ret_dk]
            r = zh * cos2 + pltpu.roll(zh, ret_dk // 2, 1) * sin2
            out_ref[:, h * ret_dk:(h + 1) * ret_dk] = r if scale is None else r * scale

    rotary(seg("qr"), qr_ref, None)
    rotary(seg("kr"), kr_ref, kr_scale)
    vr_ref[...] = seg("vr")
    gr_ref[...] = seg("gr")
    qm_ref[...] = (seg("qm") * qm_scale).astype(qm_ref.dtype)
    gt_ref[...] = _sigmoid(seg("gt"))


def _project(x2d, g, w_packed, cos2, sin2, lay, dims, q_dtype):
    t, d = x2d.shape
    tm = min(PROJ_ROWS, t)
    n_pos_blocks = cos2.shape[0] // tm
    widths = {k: hi - lo for k, (lo, hi) in lay.items()}
    out_defs = [("qa", widths["qa"], q_dtype), ("ka", widths["ka"], F32), ("va", widths["va"], F32),
                ("qi", widths["qi"], q_dtype), ("ki", dims["idx_dim"], F32), ("wi", IDX_HEADS, F32),
                ("qr", widths["qr"], F32), ("kr", widths["kr"], F32), ("vr", widths["vr"], F32),
                ("gr", widths["gr"], F32), ("qm", widths["qm"], q_dtype), ("gt", widths["gt"], F32)]
    row = lambda w: pl.BlockSpec((tm, w), lambda i: (i, 0))
    kern = functools.partial(
        _proj_kernel, lay=lay, idx_dim=dims["idx_dim"], ret_dk=dims["ret_dk"],
        qa_scale=dims["head_dim"] ** -0.5 * LOG2E, wi_scale=(IDX_HEADS ** -0.5) * (dims["idx_dim"] ** -0.5),
        kr_scale=dims["ret_dk"] ** -0.5, qm_scale=dims["mem_dim"] ** -0.5)
    outs = pl.pallas_call(
        kern,
        grid=(t // tm,),
        in_specs=[row(d), pl.BlockSpec((1, d), lambda i: (0, 0)), _whole_vmem(),
                  pl.BlockSpec((tm, LANES), lambda i: (i % n_pos_blocks, 0)),
                  pl.BlockSpec((tm, LANES), lambda i: (i % n_pos_blocks, 0))],
        out_specs=[row(w) for _, w, _ in out_defs],
        out_shape=[jax.ShapeDtypeStruct((t, w), dt) for _, w, dt in out_defs],
        compiler_params=_params(),
        name="proj",
    )(x2d, g.reshape(1, d), w_packed, cos2, sin2)
    return {name: o for (name, _, _), o in zip(out_defs, outs)}


def _rotary_tables(pos, ret_dk, rows):
    half = ret_dk // 2
    inv = ROPE_BASE ** (-jnp.arange(half, dtype=F32) / half)
    ang = pos.astype(F32)[:, None] * inv[None, :]
    cos, sin = jnp.cos(ang), jnp.sin(ang)
    cos2 = jnp.concatenate([cos, cos], axis=1)
    sin2 = jnp.concatenate([-sin, sin], axis=1)
    reps = max(1, rows // cos2.shape[0])
    return jnp.tile(cos2, (reps, 1)), jnp.tile(sin2, (reps, 1))


def _index_scores(s4, w):
    acc = w[:, 0:1] * jnp.maximum(s4[0], 0.0)
    for h in range(1, IDX_HEADS):
        acc = acc + w[:, h:h + 1] * jnp.maximum(s4[h], 0.0)
    return jnp.where(jnp.abs(acc) < MIN_NORMAL, 0.0, acc)


def _sortable_key(bits):
    return jnp.where(bits >= 0, bits, bits ^ 0x7FFFFFFF)


def _count_chunk(rows):
    return max(LANES, COUNT_ELEMS // rows)


def _tree_sum(parts):
    while len(parts) > 1:
        parts = [a + b for a, b in zip(parts[::2], parts[1::2])] + parts[len(parts) & ~1:]
    return parts[0]


def _count_ge(keys_ref, rows, n_chunks, cand):
    cw = _count_chunk(rows)
    cb = jnp.broadcast_to(cand, (rows, LANES))

    def body(c, a):
        blk = keys_ref[:, pl.ds(pl.multiple_of(c * cw, cw), cw)]
        return a + _tree_sum([jnp.where(blk[:, u * LANES:(u + 1) * LANES] >= cb, 1.0, 0.0)
                              for u in range(cw // LANES)])

    a = lax.fori_loop(0, n_chunks, body, jnp.zeros((rows, LANES), F32), unroll=isinstance(n_chunks, int))
    return jnp.sum(a, axis=1, keepdims=True)


def _kth_largest_key(keys_ref, rows, n_chunks, topk):
    def bit_step(b, u):
        cand_u = u | lax.shift_left(jnp.int32(1), 31 - b)
        return jnp.where(_count_ge(keys_ref, rows, n_chunks, cand_u ^ INT_MIN) >= topk, cand_u, u)

    return lax.fori_loop(0, 32, bit_step, jnp.zeros((rows, 1), I32)) ^ INT_MIN


def _tie_quota(thr, n_greater, topk):
    return jnp.where(thr == INT_MIN, 0.0, topk - n_greater)


def _topk_threshold(keys_ref, rows, n_chunks, topk):
    thr = _kth_largest_key(keys_ref, rows, n_chunks, topk)
    return thr, _tie_quota(thr, _count_ge(keys_ref, rows, n_chunks, thr + 1), topk)


def _lane_top_insert(cand_ref, sc_ref, rows, width):
    depth = cand_ref.shape[0]

    def body(g, c):
        for sub in range(2):
            r = pl.ds(pl.multiple_of(g * 16 + sub * 8, 8), 8)
            tops = [cand_ref[m, r, :] for m in range(depth)]
            for c0 in range(0, width, LANES):
                x = sc_ref[r, c0:c0 + LANES]
                for m in range(depth):
                    tops[m], x = jnp.maximum(tops[m], x), jnp.minimum(tops[m], x)
            for m in range(depth):
                cand_ref[m, r, :] = tops[m]
        return c

    lax.fori_loop(0, rows // 16, body, 0)


def _selection_bias(keys, thr, need, run, tri):
    tw = tri.shape[0]
    eq = keys == thr
    gt = keys > thr
    eq_b = jnp.where(eq, 1.0, 0.0).astype(BF16)
    parts = []
    for c in range(keys.shape[1] // tw):
        pre = _dot(eq_b[:, c * tw:(c + 1) * tw], tri) + run
        run = pre[:, tw - 1:]
        parts.append(pre)
    prefix = parts[0] if len(parts) == 1 else jnp.concatenate(parts, axis=1)
    bias = jnp.where(gt, 0.0, jnp.where(eq, jnp.where(prefix <= need, 0.0, NEG), NEG))
    return bias, run


def _lane_tile(x, width):
    reps = width // x.shape[1]
    return x if reps == 1 else jnp.concatenate([x] * reps, axis=1)


def _softmax_tile(s, pv, m_ref, l_ref, acc_ref, idx):
    m_prev = m_ref[idx]
    m_new = jnp.maximum(m_prev, jnp.max(s, axis=1, keepdims=True))
    alpha = jnp.exp2(m_prev - m_new)
    p = jnp.exp2(s - _lane_tile(m_new, s.shape[1]))
    l_ref[idx] = alpha * l_ref[idx] + jnp.sum(p, axis=1, keepdims=True)
    acc = acc_ref[idx]
    acc_ref[idx] = _lane_tile(alpha, acc.shape[1]) * acc + pv(p.astype(BF16))
    m_ref[idx] = m_new


def _sparse_prompt_kernel(qi_ref, wi_ref, qa_ref, kit_ref, kt_ref, v_ref, tri_ref, o_ref,
                          keys_ref, sc_ref, cand_ref, ckeys_ref, p_ref, m_ref, acc_ref,
                          *, tq, kb, topk, kv_heads, group, head_dim):
    i = pl.program_id(1)
    n_tiles = ((i + 1) * tq + kb - 1) // kb
    row_pos = i * tq + lax.broadcasted_iota(I32, (tq, kb), 0)
    col_iota = lax.broadcasted_iota(I32, (tq, kb), 1)
    w = wi_ref[...]
    qi = qi_ref[...].reshape(IDX_HEADS * tq, qi_ref.shape[2])
    cw = _count_chunk(tq)
    n_chunks = n_tiles * (kb // cw)
    depth = cand_ref.shape[0]
    cand_chunks = depth * LANES // cw
    prefilter = n_chunks >= PREFILTER_MIN_CHUNKS

    @pl.when(prefilter)
    def _():
        cand_ref[...] = jnp.full(cand_ref.shape, -jnp.inf, F32)

    def score_tile(j, c, diagonal):
        k0 = pl.multiple_of(j * kb, kb)
        s4 = _dot(qi, kit_ref[0, :, pl.ds(k0, kb)]).reshape(IDX_HEADS, tq, kb)
        sc = _index_scores(s4, w)
        key = _sortable_key(lax.bitcast_convert_type(sc, I32))
        if diagonal:
            visible = k0 + col_iota <= row_pos
            key = jnp.where(visible, key, INT_MIN)
            sc = jnp.where(visible, sc, -jnp.inf)
        keys_ref[:, pl.ds(k0, kb)] = key

        @pl.when(prefilter)
        def _():
            sc_ref[...] = sc
            _lane_top_insert(cand_ref, sc_ref, tq, kb)

        return c

    lax.fori_loop(0, n_tiles - 1, functools.partial(score_tile, diagonal=False), 0)
    score_tile(n_tiles - 1, 0, diagonal=True)

    def ranked_directly():
        return _topk_threshold(keys_ref, tq, n_chunks, topk)

    def ranked_from_lane_tops():
        for m in range(depth):
            ckeys_ref[:, m * LANES:(m + 1) * LANES] = _sortable_key(lax.bitcast_convert_type(cand_ref[m], I32))
        thr = _kth_largest_key(ckeys_ref, tq, cand_chunks, topk)
        n_greater = _count_ge(keys_ref, tq, n_chunks, thr + 1)
        kept = _count_ge(ckeys_ref, tq, cand_chunks, thr + 1)
        missed = jnp.max(n_greater - kept)
        return lax.cond(missed == 0.0, lambda: (thr, _tie_quota(thr, n_greater, topk)), ranked_directly)

    thr, need = lax.cond(prefilter, ranked_from_lane_tops, ranked_directly)

    m_ref[...] = jnp.full(m_ref.shape, NEG, F32)
    acc_ref[...] = jnp.zeros(acc_ref.shape, F32)

    def attn_tile(j, run):
        k0 = pl.multiple_of(j * kb, kb)
        bias, run = _selection_bias(keys_ref[:, pl.ds(k0, kb)], thr, need, run, tri_ref[...])
        gr = group * tq
        for pair in range(kv_heads // 2):
            alphas = []
            for half in range(2):
                n = 2 * pair + half
                qn = qa_ref[n * group:(n + 1) * group].reshape(gr, head_dim)
                s = _dot(qn, kt_ref[0, n * head_dim:(n + 1) * head_dim, pl.ds(k0, kb)])
                s = (s.reshape(group, tq, kb) + bias[None]).reshape(gr, kb)
                m_prev = m_ref[n]
                m_new = jnp.maximum(m_prev, jnp.max(s, axis=1, keepdims=True))
                alphas.append(jnp.exp2(m_prev - m_new))
                p_ref[half * gr:(half + 1) * gr, :] = jnp.exp2(s - _lane_tile(m_new, kb)).astype(BF16)
                m_ref[n] = m_new
            pv = _dot(p_ref[...], v_ref[0, pl.ds(k0, kb), pair * 2 * LANES:(pair + 1) * 2 * LANES])
            for half in range(2):
                n = 2 * pair + half
                acc_ref[n] = alphas[half] * acc_ref[n] + pv[half * gr:(half + 1) * gr, half * LANES:(half + 1) * LANES]
        return run

    lax.fori_loop(0, n_tiles, attn_tile, jnp.zeros((tq, 1), F32))

    for n in range(kv_heads):
        acc = acc_ref[n]
        o = acc * pltpu.roll(1.0 / acc, LANES - head_dim, 1)
        for g in range(group):
            h = n * group + g
            o_ref[:, h * head_dim:(h + 1) * head_dim] = o[g * tq:(g + 1) * tq, :head_dim]


def _sparse_attention_prompt(qi, wi, qa, kit, kt, v, tri, topk, kv_heads, head_dim):
    q_heads = qa.shape[0]
    b, s, _ = v.shape
    tq, kb = min(ATTN_TQ, s), min(ATTN_KB, s)
    nq = s // tq
    group = q_heads // kv_heads
    kern = functools.partial(_sparse_prompt_kernel, tq=tq, kb=kb, topk=topk, kv_heads=kv_heads,
                             group=group, head_dim=head_dim)
    resident = lambda shape: pl.BlockSpec((1,) + shape, lambda bi, i: (bi, 0, 0), pipeline_mode=pl.Buffered(1))
    return pl.pallas_call(
        kern,
        grid=(b, nq),
        in_specs=[pl.BlockSpec((IDX_HEADS, tq, qi.shape[2]), lambda bi, i: (0, bi * nq + i, 0)),
                  pl.BlockSpec((tq, IDX_HEADS), lambda bi, i: (bi * nq + i, 0)),
                  pl.BlockSpec((q_heads, tq, head_dim), lambda bi, i: (0, bi * nq + i, 0)),
                  resident(kit.shape[1:]), resident(kt.shape[1:]), resident(v.shape[1:]), _whole_vmem()],
        out_specs=pl.BlockSpec((tq, q_heads * head_dim), lambda bi, i: (bi * nq + i, 0)),
        out_shape=jax.ShapeDtypeStruct((b * s, q_heads * head_dim), F32),
        scratch_shapes=[pltpu.VMEM((tq, s), I32),
                        pltpu.VMEM((tq, kb), F32),
                        pltpu.VMEM((PREFILTER_DEPTH, tq, LANES), F32),
                        pltpu.VMEM((tq, PREFILTER_DEPTH * LANES), I32),
                        pltpu.VMEM((2 * group * tq, kb), BF16),
                        pltpu.VMEM((kv_heads, group * tq, LANES), F32),
                        pltpu.VMEM((kv_heads, group * tq, LANES), F32)],
        compiler_params=_params(2),
        name="sparse_attn_prompt",
    )(qi, wi, qa, kit, kt, v, tri)


def _sparse_sample_kernel(pt_ref, qi_ref, wi_ref, qbd_ref, kin_ref, kan_ref, van_ref, tri_ref,
                          cidx_hbm, ck_hbm, cv_hbm, o_ref,
                          ibuf, kbuf, vbuf, sems, keys_ref, m_ref, l_ref, acc_ref,
                          *, t, kb, topk, n_pages, page, kv_heads, group, head_dim):
    b = pl.program_id(0)
    nb = pl.num_programs(0)
    past = n_pages * page
    rows = kv_heads * group * t
    cw = _count_chunk(t)

    def page_copies(bb, slot, p):
        pg = pt_ref[bb, p]
        dst = pl.ds(pl.multiple_of(p * page, page), page)
        return (pltpu.make_async_copy(cidx_hbm.at[pg], ibuf.at[slot, :, dst], sems.at[slot, 0]),
                pltpu.make_async_copy(ck_hbm.at[pg], kbuf.at[slot, :, dst], sems.at[slot, 1]),
                pltpu.make_async_copy(cv_hbm.at[pg], vbuf.at[slot, :, dst], sems.at[slot, 2]))

    def start_fetch(bb, slot):
        def body(p, c):
            for cp in page_copies(bb, slot, p):
                cp.start()
            return c
        lax.fori_loop(0, n_pages, body, 0)

    def wait_fetch(bb, slot):
        def body(p, c):
            for cp in page_copies(bb, slot, p):
                cp.wait()
            return c
        lax.fori_loop(0, n_pages, body, 0)

    slot = b % 2

    @pl.when(b == 0)
    def _():
        start_fetch(b, slot)

    @pl.when(b + 1 < nb)
    def _():
        start_fetch(b + 1, 1 - slot)

    wait_fetch(b, slot)

    w = wi_ref[0]
    qi = qi_ref[0]
    n_tiles = past // kb

    def score_tile(j, c):
        k0 = pl.multiple_of(j * kb, kb)
        s4 = _dot(qi, ibuf[slot, :, pl.ds(k0, kb)].astype(BF16)).reshape(IDX_HEADS, t, kb)
        keys_ref[:, pl.ds(k0, kb)] = _sortable_key(lax.bitcast_convert_type(_index_scores(s4, w), I32))
        return c

    lax.fori_loop(0, n_tiles, score_tile, 0)
    s4 = _dot_nt(qi, kin_ref[0]).reshape(IDX_HEADS, t, LANES)
    key_new = _sortable_key(lax.bitcast_convert_type(_index_scores(s4, w), I32))
    new_col = lax.broadcasted_iota(I32, (t, LANES), 1)
    new_row = lax.broadcasted_iota(I32, (t, LANES), 0)
    keys_ref[:, past:past + LANES] = jnp.where(new_col <= new_row, key_new, INT_MIN)
    keys_ref[:, past + LANES:past + cw] = jnp.full((t, cw - LANES), INT_MIN, I32)

    thr, need = _topk_threshold(keys_ref, t, past // cw + 1, topk)

    m_ref[...] = jnp.full(m_ref.shape, NEG, F32)
    l_ref[...] = jnp.zeros(l_ref.shape, F32)
    acc_ref[...] = jnp.zeros(acc_ref.shape, F32)
    qbd = qbd_ref[0]

    def tile_bias(bias, width):
        return jnp.broadcast_to(bias[None], (rows // t, t, width)).reshape(rows, width)

    def attn_tile(j, run):
        k0 = pl.multiple_of(j * kb, kb)
        bias, run = _selection_bias(keys_ref[:, pl.ds(k0, kb)], thr, need, run, tri_ref[...])
        s = _dot(qbd, kbuf[slot, :, pl.ds(k0, kb)].astype(BF16)) + tile_bias(bias, kb)
        vt = vbuf[slot, :, pl.ds(k0, kb)].astype(BF16)
        _softmax_tile(s, lambda p: _dot_nt(p, vt), m_ref, l_ref, acc_ref, 0)
        return run

    run = lax.fori_loop(0, n_tiles, attn_tile, jnp.zeros((t, 1), F32))
    bias, _ = _selection_bias(keys_ref[:, past:past + LANES], thr, need, run, tri_ref[:LANES, :LANES])
    s = _dot_nt(qbd, kan_ref[0]) + tile_bias(bias, LANES)
    van = van_ref[0]
    _softmax_tile(s, lambda p: _dot(p, van), m_ref, l_ref, acc_ref, 0)

    o = acc_ref[0] * (1.0 / _lane_tile(l_ref[0], acc_ref.shape[2]))
    for n in range(kv_heads):
        for g in range(group):
            h = n * group + g
            r0 = (n * group + g) * t
            o_ref[0, :, h * head_dim:(h + 1) * head_dim] = o[r0:r0 + t, n * head_dim:(n + 1) * head_dim]


def _sparse_attention_sample(page_table, qi, wi, qbd, ki_new, ka_new, va_new, tri, cache_idx_t, cache_k_t, cache_v_t,
                             topk, kv_heads, group, head_dim):
    db, t, _ = wi.shape
    n_pages = page_table.shape[1]
    page = cache_idx_t.shape[2]
    past = n_pages * page
    kb = min(SAMPLE_KB, past)
    rows = kv_heads * group * t
    kvw = kv_heads * head_dim
    cw = _count_chunk(t)
    kern = functools.partial(_sparse_sample_kernel, t=t, kb=kb, topk=topk, n_pages=n_pages, page=page,
                             kv_heads=kv_heads, group=group, head_dim=head_dim)
    per_b = lambda shape: pl.BlockSpec((1,) + shape, lambda b, pt: (b, 0, 0))
    grid_spec = pltpu.PrefetchScalarGridSpec(
        num_scalar_prefetch=1,
        grid=(db,),
        in_specs=[per_b(qi.shape[1:]), per_b(wi.shape[1:]), per_b(qbd.shape[1:]), per_b(ki_new.shape[1:]),
                  per_b(ka_new.shape[1:]), per_b(va_new.shape[1:]),
                  pl.BlockSpec(tri.shape, lambda b, pt: (0, 0)),
                  pl.BlockSpec(memory_space=pl.ANY), pl.BlockSpec(memory_space=pl.ANY),
                  pl.BlockSpec(memory_space=pl.ANY)],
        out_specs=per_b((t, kv_heads * group * head_dim)),
        scratch_shapes=[pltpu.VMEM((2, cache_idx_t.shape[1], past), F32),
                        pltpu.VMEM((2, kvw, past), F32),
                        pltpu.VMEM((2, kvw, past), F32),
                        pltpu.SemaphoreType.DMA((2, 3)),
                        pltpu.VMEM((t, past + cw), I32),
                        pltpu.VMEM((1, rows, LANES), F32),
                        pltpu.VMEM((1, rows, LANES), F32),
                        pltpu.VMEM((1, rows, kvw), F32)])
    return pl.pallas_call(
        kern,
        grid_spec=grid_spec,
        out_shape=jax.ShapeDtypeStruct((db, t, kv_heads * group * head_dim), F32),
        compiler_params=_params(),
        name="sparse_attn_sample",
    )(page_table, qi, wi, qbd, ki_new, ka_new, va_new, tri, cache_idx_t, cache_k_t, cache_v_t)


def _log_gamma(heads):
    return np.log(1.0 - np.exp2(-RET_DECAY_OFFSET - np.arange(heads, dtype=np.float64)))


def _retention_tables(heads, chunk, dk, group):
    lg = _log_gamma(heads)
    i = np.arange(chunk, dtype=np.float64)
    diff = i[:, None] - i[None, :]
    decay = np.where(diff >= 0, np.exp(lg[:, None, None] * np.maximum(diff, 0.0)), 0.0)
    decay_bd = np.zeros((heads, group * chunk, group * chunk))
    for gidx in range(group):
        decay_bd[:, gidx * chunk:(gidx + 1) * chunk, gidx * chunk:(gidx + 1) * chunk] = decay
    q_scale = np.tile(np.exp(lg[:, None] * (i + 1.0)), (1, group))
    k_scale = np.tile(np.exp(lg[:, None] * (chunk - 1.0 - i)), (1, group))
    state_scale = np.exp(lg * chunk)
    rows_q = np.broadcast_to(q_scale[:, :, None], (heads, group * chunk, dk))
    rows_k = np.broadcast_to(k_scale[:, :, None], (heads, group * chunk, dk))
    cols_k = np.broadcast_to(k_scale[:, None, :], (heads, dk, group * chunk))
    as32 = lambda a: jnp.asarray(np.ascontiguousarray(a), F32)
    return as32(decay_bd), as32(rows_q), as32(rows_k), as32(cols_k), [float(v) for v in state_scale]


def _groupnorm_gate(o, gn_g, g_r):
    mu = jnp.mean(o, axis=-1, keepdims=True)
    d = o - mu
    var = jnp.mean(d * d, axis=-1, keepdims=True)
    return (g_r * _sigmoid(g_r)) * ((d * lax.rsqrt(var + EPS)) * gn_g)


def _retention_prompt_kernel(q_ref, k_ref, v_ref, g_ref, gn_ref, dec_ref, qs_ref, ks_ref, o_ref, st_ref, state,
                             *, heads, dk, dv, state_scale):
    c = pl.program_id(0)

    @pl.when(c == 0)
    def _():
        state[...] = jnp.zeros(state.shape, F32)

    for bi in range(q_ref.shape[0]):
        for h in range(heads):
            q = q_ref[bi, :, h * dk:(h + 1) * dk]
            k = k_ref[bi, :, h * dk:(h + 1) * dk]
            v = v_ref[bi, :, h * dv:(h + 1) * dv].astype(BF16)
            s_old = state[bi, h]
            inner = _dot_nt(q.astype(BF16), k.astype(BF16)) * dec_ref[h]
            o = _dot(inner.astype(BF16), v) + _dot((q * qs_ref[h]).astype(BF16), s_old.astype(BF16))
            state[bi, h] = state_scale[h] * s_old + _dot((k * ks_ref[h]).T.astype(BF16), v)
            o_ref[bi, :, h * dv:(h + 1) * dv] = _groupnorm_gate(o, gn_ref[:, h * dv:(h + 1) * dv],
                                                                g_ref[bi, :, h * dv:(h + 1) * dv])

    @pl.when(c == pl.num_programs(0) - 1)
    def _():
        st_ref[...] = state[...]


def _retention_prompt(q, k, v, g, gn_g, heads, dk, dv):
    b, s, _ = q.shape
    chunk = min(RET_CHUNK, s)
    dec, qs, ks, _, state_scale = _retention_tables(heads, chunk, dk, 1)
    blk = lambda w: pl.BlockSpec((b, chunk, w), lambda ci: (0, ci, 0))
    const = lambda a: pl.BlockSpec(a.shape, lambda ci: (0,) * a.ndim)
    gn2 = gn_g.reshape(1, heads * dv)
    kern = functools.partial(_retention_prompt_kernel, heads=heads, dk=dk, dv=dv, state_scale=state_scale)
    return pl.pallas_call(
        kern,
        grid=(s // chunk,),
        in_specs=[blk(heads * dk), blk(heads * dk), blk(heads * dv), blk(heads * dv),
                  const(gn2), const(dec), const(qs), const(ks)],
        out_specs=[blk(heads * dv), pl.BlockSpec((b, heads, dk, dv), lambda ci: (0, 0, 0, 0))],
        out_shape=[jax.ShapeDtypeStruct((b, s, heads * dv), F32),
                   jax.ShapeDtypeStruct((b, heads, dk, dv), F32)],
        scratch_shapes=[pltpu.VMEM((b, heads, dk, dv), F32)],
        compiler_params=_params(),
        name="retention_prompt",
    )(q, k, v, g, gn2, dec, qs, ks)


def _retention_sample_kernel(q_ref, k_ref, kt_ref, v_ref, g_ref, gn_ref, dec_ref, qs_ref, kst_ref, st_ref,
                             o_ref, sto_ref, *, heads, dk, dv, t, group, state_scale):
    rows = group * t
    col_batch = lax.broadcasted_iota(I32, (dk, rows), 1) // t
    for h in range(heads):
        q = q_ref[:, h * dk:(h + 1) * dk]
        k = k_ref[:, h * dk:(h + 1) * dk]
        v = v_ref[:, h * dv:(h + 1) * dv].astype(BF16)
        inner = _dot_nt(q.astype(BF16), k.astype(BF16)) * dec_ref[h]
        o = _dot(inner.astype(BF16), v)
        qd = (q * qs_ref[h]).astype(BF16)
        kdt = kt_ref[0, h] * kst_ref[h]
        cross = []
        for bi in range(group):
            s_old = st_ref[bi, h]
            cross.append(_dot(qd, s_old.astype(BF16))[bi * t:(bi + 1) * t])
            kd_b = jnp.where(col_batch == bi, kdt, 0.0).astype(BF16)
            sto_ref[bi, h] = state_scale[h] * s_old + _dot(kd_b, v)
        o = o + jnp.concatenate(cross, axis=0)
        o_ref[:, h * dv:(h + 1) * dv] = _groupnorm_gate(o, gn_ref[:, h * dv:(h + 1) * dv],
                                                        g_ref[:, h * dv:(h + 1) * dv])


def _retention_sample(state, q, k, v, g, gn_g, heads, dk, dv, t):
    db = state.shape[0]
    group = min(SAMPLE_GROUP, db)
    rows = group * t
    dec, qs, _, kst, state_scale = _retention_tables(heads, t, dk, group)
    kt = k.reshape(db // group, rows, heads, dk).transpose(0, 2, 3, 1)
    gn2 = gn_g.reshape(1, heads * dv)
    blk = lambda w: pl.BlockSpec((rows, w), lambda i: (i, 0))
    const = lambda a: pl.BlockSpec(a.shape, lambda i: (0,) * a.ndim)
    st_spec = pl.BlockSpec((group, heads, dk, dv), lambda i: (i, 0, 0, 0))
    kern = functools.partial(_retention_sample_kernel, heads=heads, dk=dk, dv=dv, t=t, group=group,
                             state_scale=state_scale)
    return pl.pallas_call(
        kern,
        grid=(db // group,),
        in_specs=[blk(heads * dk), blk(heads * dk),
                  pl.BlockSpec((1, heads, dk, rows), lambda i: (i, 0, 0, 0)),
                  blk(heads * dv), blk(heads * dv), const(gn2), const(dec), const(qs), const(kst), st_spec],
        out_specs=[blk(heads * dv), st_spec],
        out_shape=[jax.ShapeDtypeStruct((db * t, heads * dv), F32),
                   jax.ShapeDtypeStruct(state.shape, F32)],
        compiler_params=_params(),
        name="retention_sample",
    )(q, k, kt, v, g, gn2, dec, qs, kst, state)


def _mem_kv_kernel(x_ref, g_ref, w_ref, k_ref, v_ref):
    hb = _rmsnorm(x_ref[...], g_ref[...]).astype(BF16)
    half = w_ref.shape[1] // 2
    k_ref[...] = _dot(hb, w_ref[:, :half])
    v_ref[...] = _dot(hb, w_ref[:, half:])


def _memory_kv(mem2d, g, w_bf16):
    t, d = mem2d.shape
    half = w_bf16.shape[1] // 2
    tm = min(PROJ_ROWS, t)
    return pl.pallas_call(
        _mem_kv_kernel,
        grid=(t // tm,),
        in_specs=[pl.BlockSpec((tm, d), lambda i: (i, 0)), pl.BlockSpec((1, d), lambda i: (0, 0)), _whole_vmem()],
        out_specs=[pl.BlockSpec((tm, half), lambda i: (i, 0))] * 2,
        out_shape=[jax.ShapeDtypeStruct((t, half), F32)] * 2,
        compiler_params=_params(),
        name="memory_kv",
    )(mem2d, g.reshape(1, d), w_bf16)


def _mem_head_attention(q, k, v):
    s = _dot_nt(q, k)
    p = jnp.exp(s - jnp.max(s, axis=1, keepdims=True))
    return _dot(p.astype(BF16), v) * (1.0 / jnp.sum(p, axis=1, keepdims=True))


def _mem_attn_prompt_kernel(q_ref, k_ref, v_ref, o_ref, *, heads, dim):
    for h in range(heads):
        sl = slice(h * dim, (h + 1) * dim)
        o_ref[0, :, sl] = _mem_head_attention(q_ref[0, :, sl], k_ref[0, :, sl], v_ref[0, :, sl])


def _memory_attention_prompt(q, mk, mv, heads, dim):
    b, s, w = q.shape
    tm = min(MEM_ROWS, s)
    m = mk.shape[1]
    return pl.pallas_call(
        functools.partial(_mem_attn_prompt_kernel, heads=heads, dim=dim),
        grid=(b, s // tm),
        in_specs=[pl.BlockSpec((1, tm, w), lambda bi, i: (bi, i, 0)),
                  pl.BlockSpec((1, m, w), lambda bi, i: (bi, 0, 0)),
                  pl.BlockSpec((1, m, w), lambda bi, i: (bi, 0, 0))],
        out_specs=pl.BlockSpec((1, tm, w), lambda bi, i: (bi, i, 0)),
        out_shape=jax.ShapeDtypeStruct((b, s, w), F32),
        compiler_params=_params(2),
        name="memory_attn_prompt",
    )(q, mk, mv)


def _mem_attn_sample_kernel(q_ref, k_ref, v_ref, o_ref, *, heads, dim, t, group):
    for bi in range(group):
        for h in range(heads):
            sl = slice(h * dim, (h + 1) * dim)
            o = _mem_head_attention(q_ref[:, sl].astype(BF16), k_ref[bi, :, sl], v_ref[bi, :, sl])
            o_ref[bi * t:(bi + 1) * t, sl] = o[bi * t:(bi + 1) * t]


def _memory_attention_sample(q, mk, mv, heads, dim, t):
    db, m, w = mk.shape
    group = min(SAMPLE_GROUP, db)
    mem_spec = pl.BlockSpec((group, m, w), lambda i: (i, 0, 0))
    return pl.pallas_call(
        functools.partial(_mem_attn_sample_kernel, heads=heads, dim=dim, t=t, group=group),
        grid=(db // group,),
        in_specs=[pl.BlockSpec((group * t, w), lambda i: (i, 0)), mem_spec, mem_spec],
        out_specs=pl.BlockSpec((group * t, w), lambda i: (i, 0)),
        out_shape=jax.ShapeDtypeStruct((db * t, w), F32),
        compiler_params=_params(),
        name="memory_attn_sample",
    )(q, mk, mv)


def _merge_ffn_kernel(x_ref, gt_ref, oa_ref, or_ref, om_ref, wo_ref, fg_ref, wg_ref, wu_ref, wd_ref, fin_ref,
                      y_ref, *, final_norm):
    d = x_ref.shape[1]
    u = (gt_ref[:, :d] * oa_ref[...] + gt_ref[:, d:2 * d] * or_ref[...] + gt_ref[:, 2 * d:] * om_ref[...])
    x1 = x_ref[...] + _dot(u.astype(BF16), wo_ref[...])
    hb = _rmsnorm(x1, fg_ref[...]).astype(BF16)
    a = _dot(hb, wg_ref[...])
    act = (a * _sigmoid(a)) * _dot(hb, wu_ref[...])
    x2 = x1 + _dot(act.astype(BF16), wd_ref[...])
    y_ref[...] = _rmsnorm(x2, fin_ref[...]) if final_norm else x2


def _merge_ffn(x2d, gates, o_a, o_r, o_m, w_out, ffn_g, w_gate, w_up, w_down, final_g, final_norm):
    t, d = x2d.shape
    tm = min(PROJ_ROWS, t)
    row = lambda w: pl.BlockSpec((tm, w), lambda i: (i, 0))
    vec = pl.BlockSpec((1, d), lambda i: (0, 0))
    return pl.pallas_call(
        functools.partial(_merge_ffn_kernel, final_norm=final_norm),
        grid=(t // tm,),
        in_specs=[row(d), row(gates.shape[1]), row(d), row(d), row(d), _whole_vmem(), vec,
                  _whole_vmem(), _whole_vmem(), _whole_vmem(), vec],
        out_specs=row(d),
        out_shape=jax.ShapeDtypeStruct((t, d), F32),
        compiler_params=_params(),
        name="merge_ffn",
    )(x2d, gates, o_a, o_r, o_m, w_out, ffn_g.reshape(1, d), w_gate, w_up, w_down, final_g.reshape(1, d))


def kernel(x_prompt, x_sample, cache_k, cache_v, cache_idx_k, state_ret, cache_mem_k, cache_mem_v, page_table,
           mem_prompt, attn_norm_g, w_in, ret_gn_g, w_out, ffn_norm_g, w_gate, w_up, w_down, mem_norm_g, w_mem_kv,
           final_norm_g):
    depth = w_in.shape[0]
    b, s, d = x_prompt.shape
    db, t, _ = x_sample.shape
    _, n_pool, page, kv_heads, head_dim = cache_k.shape
    idx_dim = cache_idx_k.shape[-1]
    _, _, ret_heads, ret_dk, ret_dv = state_ret.shape
    _, _, mem_tokens, mem_heads, mem_dim = cache_mem_k.shape
    q_heads = d // head_dim
    group = q_heads // kv_heads
    kvw = kv_heads * head_dim
    n_pages = page_table.shape[1]
    past = n_pages * page
    dims = dict(idx_dim=idx_dim, ret_dk=ret_dk, head_dim=head_dim, mem_dim=mem_dim)
    lay, _ = _proj_layout(d, kvw, idx_dim, ret_heads * ret_dk, ret_heads * ret_dv, mem_heads * mem_dim,
                          w_in.shape[2] - (d + 2 * kvw + IDX_HEADS * idx_dim + idx_dim + IDX_HEADS
                                           + 2 * ret_heads * ret_dk + 2 * ret_heads * ret_dv + mem_heads * mem_dim))
    o_ki = d + 2 * kvw + IDX_HEADS * idx_dim
    o_qr = o_ki + idx_dim + IDX_HEADS

    cos_p, sin_p = _rotary_tables(jnp.arange(s), ret_dk, min(PROJ_ROWS, b * s))
    cos_s, sin_s = _rotary_tables(past + jnp.arange(t), ret_dk, min(PROJ_ROWS, db * t))
    tri = (jnp.arange(TRI_W)[:, None] <= jnp.arange(TRI_W)[None, :]).astype(BF16)
    topk_p = min(TOPK_MAX, s // 4)
    topk_s = min(TOPK_MAX, (past + t) // 4)

    xp = x_prompt.reshape(b * s, d)
    xs = x_sample.reshape(db * t, d)
    outs = {k: [] for k in ("k_p", "v_p", "ik_p", "rs_p", "mk_p", "mv_p", "k_s", "v_s", "ik_s", "rs_s")}
    for l in range(depth):
        last = l == depth - 1
        wl = w_in[l]
        w_packed = jnp.concatenate(
            [wl[:, :o_ki], wl[:, o_ki:o_qr], jnp.zeros((d, LANES - (o_qr - o_ki)), wl.dtype), wl[:, o_qr:]],
            axis=1).astype(BF16)
        w_out_b, w_gate_b, w_up_b, w_down_b = (a[l].astype(BF16) for a in (w_out, w_gate, w_up, w_down))

        p = _project(xp, attn_norm_g[l], w_packed, cos_p, sin_p, lay, dims, BF16)
        qi = p["qi"].reshape(b * s, IDX_HEADS, idx_dim).transpose(1, 0, 2)
        qa = p["qa"].reshape(b * s, q_heads, head_dim).transpose(1, 0, 2)
        kit = p["ki"].astype(BF16).reshape(b, s, idx_dim).transpose(0, 2, 1)
        kt = p["ka"].astype(BF16).reshape(b, s, kvw).transpose(0, 2, 1)
        va = p["va"].astype(BF16).reshape(b, s, kv_heads, head_dim)
        vb = jnp.concatenate([va, jnp.ones((b, s, kv_heads, LANES - head_dim), BF16)], axis=-1)
        o_a = _sparse_attention_prompt(qi, p["wi"], qa, kit, kt, vb.reshape(b, s, kv_heads * LANES), tri, topk_p,
                                       kv_heads, head_dim)
        o_r, st_p = _retention_prompt(p["qr"].reshape(b, s, -1), p["kr"].reshape(b, s, -1),
                                      p["vr"].reshape(b, s, -1), p["gr"].reshape(b, s, -1), ret_gn_g[l],
                                      ret_heads, ret_dk, ret_dv)
        mk, mv = _memory_kv(mem_prompt.reshape(b * mem_tokens, d), mem_norm_g[l], w_mem_kv[l].astype(BF16))
        o_m = _memory_attention_prompt(p["qm"].reshape(b, s, -1), mk.astype(BF16).reshape(b, mem_tokens, -1),
                                       mv.astype(BF16).reshape(b, mem_tokens, -1), mem_heads, mem_dim)
        xp = _merge_ffn(xp, p["gt"], o_a, o_r.reshape(b * s, -1), o_m.reshape(b * s, -1), w_out_b, ffn_norm_g[l],
                        w_gate_b, w_up_b, w_down_b, final_norm_g, last)
        outs["k_p"].append(p["ka"].reshape(b, s, kv_heads, head_dim))
        outs["v_p"].append(p["va"].reshape(b, s, kv_heads, head_dim))
        outs["ik_p"].append(p["ki"].reshape(b, s, idx_dim))
        outs["rs_p"].append(st_p)
        outs["mk_p"].append(mk.reshape(b, mem_tokens, mem_heads, mem_dim))
        outs["mv_p"].append(mv.reshape(b, mem_tokens, mem_heads, mem_dim))

        q = _project(xs, attn_norm_g[l], w_packed, cos_s, sin_s, lay, dims, F32)
        qi_s = q["qi"].reshape(db, t, IDX_HEADS, idx_dim).transpose(0, 2, 1, 3).reshape(db, IDX_HEADS * t, idx_dim)
        qa_s = q["qa"].reshape(db, t, kv_heads, group, head_dim).transpose(0, 2, 3, 1, 4)
        qbd = (qa_s.reshape(db, kv_heads, group * t, 1, head_dim)
               * jnp.eye(kv_heads, dtype=F32)[None, :, None, :, None]).reshape(db, kv_heads * group * t, kvw)
        pad_rows = lambda a: jnp.pad(a.reshape(db, t, -1), ((0, 0), (0, LANES - t), (0, 0))).astype(BF16)
        o_a = _sparse_attention_sample(
            page_table, qi_s.astype(BF16), q["wi"].reshape(db, t, IDX_HEADS), qbd.astype(BF16),
            pad_rows(q["ki"]), pad_rows(q["ka"]), pad_rows(q["va"]), tri,
            jnp.swapaxes(cache_idx_k[l], 1, 2),
            cache_k[l].transpose(0, 2, 3, 1).reshape(n_pool, kvw, page),
            cache_v[l].transpose(0, 2, 3, 1).reshape(n_pool, kvw, page),
            topk_s, kv_heads, group, head_dim).reshape(db * t, d)
        o_r, st_s = _retention_sample(state_ret[l], q["qr"], q["kr"], q["vr"], q["gr"], ret_gn_g[l],
                                      ret_heads, ret_dk, ret_dv, t)
        o_m = _memory_attention_sample(q["qm"], cache_mem_k[l].reshape(db, mem_tokens, -1).astype(BF16),
                                       cache_mem_v[l].reshape(db, mem_tokens, -1).astype(BF16), mem_heads, mem_dim, t)
        xs = _merge_ffn(xs, q["gt"], o_a, o_r, o_m, w_out_b, ffn_norm_g[l], w_gate_b, w_up_b, w_down_b,
                        final_norm_g, last)
        outs["k_s"].append(q["ka"].reshape(db, t, kv_heads, head_dim))
        outs["v_s"].append(q["va"].reshape(db, t, kv_heads, head_dim))
        outs["ik_s"].append(q["ki"].reshape(db, t, idx_dim))
        outs["rs_s"].append(st_s)

    stk = lambda k: jnp.stack(outs[k])
    return (xp.reshape(b, s, d), xs.reshape(db, t, d), stk("k_p"), stk("v_p"), stk("ik_p"), stk("rs_p"),
            stk("mk_p"), stk("mv_p"), stk("k_s"), stk("v_s"), stk("ik_s"), stk("rs_s"))
```
